```python
import math
import jax, jax.numpy as jnp
from jax import lax
import numpy as np

D_MODEL = 1024
BATCH = 1
SEQ = 16384
DEPTH = 2
DEC_BATCH = 32
DEC_SEQ = 1
PAST_LEN = 16384
PAGE_SIZE = 128

SSM_WIDTH = D_MODEL // 2
SSM_GROUP = 16
SSM_GROUPS = SSM_WIDTH // SSM_GROUP
SSM_STATE = 64
N_HEADS = 8
HEAD_DIM = 64
ATT_WIDTH = N_HEADS * HEAD_DIM
IN_WIDTH = SSM_WIDTH + 3 * ATT_WIDTH
D_FF = 2816
Q_BLOCK = 128
RMS_EPS = 1e-6
DT_MIN = 0.001
DT_MAX = 0.1
SB_BIAS_INIT = -6.0

kernel_name = "hybrid_s5_stickbreaking_macaron_step"


def _rmsnorm(x, g):
    xf = x.astype(jnp.float32)
    y = xf * lax.rsqrt(jnp.mean(xf * xf, axis=-1, keepdims=True) + RMS_EPS)
    return (y * g.astype(jnp.float32)).astype(x.dtype)


def _swiglu(x, w_up, w_down):
    a, b = jnp.split(x @ w_up, 2, axis=-1)
    return (jax.nn.silu(a) * b) @ w_down


def _ffn_sublayer(x, pre_g, post_g, w_up, w_down):
    return x + 0.5 * _rmsnorm(_swiglu(_rmsnorm(x, pre_g), w_up, w_down), post_g)


def _scan_combine(e1, e2):
    a1, b1 = e1
    a2, b2 = e2
    return a1 * a2, a2 * b1 + b2


def _s5(u, h0_re, h0_im, lam_re, lam_im, log_dt, b_re, b_im, c_re, c_im, d_skip):
    f32 = jnp.float32
    bsz, t, _ = u.shape
    ug = u.astype(f32).reshape(bsz, t, SSM_GROUPS, SSM_GROUP)
    lam = lax.complex(lam_re.astype(f32), lam_im.astype(f32))
    dt = jnp.exp(log_dt.astype(f32))[:, None]
    lam_bar = jnp.exp(lam * dt)
    b_bar = ((lam_bar - 1.0) / lam)[:, :, None] * lax.complex(b_re.astype(f32), b_im.astype(f32))
    bu = jnp.einsum('btgc,gpc->btgp', ug.astype(jnp.complex64), b_bar)
    h0 = lax.complex(h0_re.astype(f32), h0_im.astype(f32))
    bu = bu.at[:, 0].add(lam_bar * h0)
    a = jnp.broadcast_to(lam_bar, bu.shape)
    _, h = lax.associative_scan(_scan_combine, (a, bu), axis=1)
    c = lax.complex(c_re.astype(f32), c_im.astype(f32))
    y = jnp.real(jnp.einsum('gcp,btgp->btgc', c, h)) + d_skip.astype(f32).reshape(SSM_GROUPS, SSM_GROUP) * ug
    h_last = h[:, -1]
    return (y.reshape(bsz, t, SSM_WIDTH).astype(u.dtype),
            jnp.real(h_last).astype(h0_re.dtype), jnp.imag(h_last).astype(h0_re.dtype))


def _stick_breaking(q, k, v, bias, q_pos, k_pos):
    z = jnp.einsum('bthd,bshd->bhts', q, k).astype(jnp.float32) * (1.0 / math.sqrt(HEAD_DIM))
    z = z + bias.astype(jnp.float32)[None, :, None, None]
    mask = k_pos[None, :] < q_pos[:, None]
    log_beta = jax.nn.log_sigmoid(z)
    log_keep = jnp.where(mask, jax.nn.log_sigmoid(-z), 0.0)
    after = lax.cumsum(log_keep, axis=3, reverse=True) - log_keep
    w = jnp.where(mask, jnp.exp(log_beta + after), 0.0)
    return jnp.einsum('bhts,bshd->bthd', w.astype(v.dtype), v)


def _sb_prompt(q, k, v, bias):
    bsz, t, h, d = q.shape
    nblk = t // Q_BLOCK
    qb = q.reshape(bsz, nblk, Q_BLOCK, h, d).swapaxes(0, 1)
    posb = jnp.arange(t).reshape(nblk, Q_BLOCK)
    kpos = jnp.arange(t)
    out = lax.map(lambda a: _stick_breaking(a[0], k, v, bias, a[1], kpos), (qb, posb))
    return out.swapaxes(0, 1).reshape(bsz, t, h, d)


def _token_mixer(x, p, h0_re, h0_im, k_past, v_past):
    u = _rmsnorm(x, p['mix_pre_g'])
    bsz, t, _ = u.shape
    proj = u @ p['w_in']
    s_in, q, k, v = jnp.split(proj, [SSM_WIDTH, SSM_WIDTH + ATT_WIDTH, SSM_WIDTH + 2 * ATT_WIDTH], axis=-1)
    q = q.reshape(bsz, t, N_HEADS, HEAD_DIM)
    k = k.reshape(bsz, t, N_HEADS, HEAD_DIM)
    v = v.reshape(bsz, t, N_HEADS, HEAD_DIM)
    y_s, h_re, h_im = _s5(s_in, h0_re, h0_im, p['lam_re'], p['lam_im'], p['log_dt'],
                          p['b_re'], p['b_im'], p['c_re'], p['c_im'], p['d_skip'])
    glu_a, glu_b = jnp.split(jax.nn.gelu(y_s) @ p['w_glu'], 2, axis=-1)
    y_s = glu_a * jax.nn.sigmoid(glu_b)
    if k_past is None:
        att = _sb_prompt(q, k, v, p['sb_bias'])
    else:
        past = k_past.shape[1]
        kk = jnp.concatenate([k_past.astype(k.dtype), k], axis=1)
        vv = jnp.concatenate([v_past.astype(v.dtype), v], axis=1)
        att = _stick_breaking(q, kk, vv, p['sb_bias'], past + jnp.arange(t), jnp.arange(past + t))
    o_s = y_s @ p['w_br_ssm']
    o_a = att.reshape(bsz, t, ATT_WIDTH) @ p['w_br_att']
    g_s, g_a = jnp.split(jax.nn.sigmoid(u @ p['w_gate'] + p['b_gate']), 2, axis=-1)
    merged = g_s * o_s + g_a * o_a
    y = _rmsnorm(merged @ p['w_out'], p['mix_post_g'])
    return x + y, k, v, h_re, h_im


def setup_inputs(seed: int = 0) -> dict:
    key = jax.random.key(seed)
    ks = iter(jax.random.split(key, 40))
    f32 = jnp.float32

    def nrm(shape, scale):
        return jax.random.normal(next(ks), shape, f32) * scale

    def gain(shape):
        return 1.0 + nrm(shape, 0.02)

    n_pages = PAST_LEN // PAGE_SIZE
    n_used = DEC_BATCH * n_pages
    n_phys = (n_used * 5) // 4
    page_table = jax.random.permutation(next(ks), n_phys)[:n_used].reshape(DEC_BATCH, n_pages).astype(jnp.int32)
    lam_im0 = jnp.broadcast_to(math.pi * jnp.arange(SSM_STATE, dtype=f32), (DEPTH, SSM_GROUPS, SSM_STATE))
    return {
        "x_prompt": nrm((BATCH, SEQ, D_MODEL), 1.0),
        "x_sample": nrm((DEC_BATCH, DEC_SEQ, D_MODEL), 1.0),
        "cache_k": nrm((DEPTH, n_phys, PAGE_SIZE, N_HEADS, HEAD_DIM), 1.0),
        "cache_v": nrm((DEPTH, n_phys, PAGE_SIZE, N_HEADS, HEAD_DIM), 1.0),
        "state_ssm_re": nrm((DEPTH, DEC_BATCH, SSM_GROUPS, SSM_STATE), 0.5),
        "state_ssm_im": nrm((DEPTH, DEC_BATCH, SSM_GROUPS, SSM_STATE), 0.5),
        "page_table": page_table,
        "ffn1_pre_g": gain((DEPTH, D_MODEL)),
        "ffn1_post_g": gain((DEPTH, D_MODEL)),
        "ffn1_w_up": nrm((DEPTH, D_MODEL, 2 * D_FF), D_MODEL ** -0.5),
        "ffn1_w_down": nrm((DEPTH, D_FF, D_MODEL), D_FF ** -0.5),
        "mix_pre_g": gain((DEPTH, D_MODEL)),
        "mix_post_g": gain((DEPTH, D_MODEL)),
        "w_in": nrm((DEPTH, D_MODEL, IN_WIDTH), D_MODEL ** -0.5),
        "sb_bias": SB_BIAS_INIT + nrm((DEPTH, N_HEADS), 0.1),
        "lam_re": -0.5 + nrm((DEPTH, SSM_GROUPS, SSM_STATE), 0.01),
        "lam_im": lam_im0 + nrm((DEPTH, SSM_GROUPS, SSM_STATE), 0.01),
        "log_dt": jax.random.uniform(next(ks), (DEPTH, SSM_GROUPS), f32, math.log(DT_MIN), math.log(DT_MAX)),
        "b_re": nrm((DEPTH, SSM_GROUPS, SSM_STATE, SSM_GROUP), (2 * SSM_GROUP) ** -0.5),
        "b_im": nrm((DEPTH, SSM_GROUPS, SSM_STATE, SSM_GROUP), (2 * SSM_GROUP) ** -0.5),
        "c_re": nrm((DEPTH, SSM_GROUPS, SSM_GROUP, SSM_STATE), (2 * SSM_STATE) ** -0.5),
        "c_im": nrm((DEPTH, SSM_GROUPS, SSM_GROUP, SSM_STATE), (2 * SSM_STATE) ** -0.5),
        "d_skip": nrm((DEPTH, SSM_WIDTH), 1.0),
        "w_glu": nrm((DEPTH, SSM_WIDTH, 2 * SSM_WIDTH), SSM_WIDTH ** -0.5),
        "w_br_ssm": nrm((DEPTH, SSM_WIDTH, D_MODEL), SSM_WIDTH ** -0.5),
        "w_br_att": nrm((DEPTH, ATT_WIDTH, D_MODEL), ATT_WIDTH ** -0.5),
        "w_gate": nrm((DEPTH, D_MODEL, 2 * D_MODEL), D_MODEL ** -0.5),
        "b_gate": nrm((DEPTH, 2 * D_MODEL), 0.02),
        "w_out": nrm((DEPTH, D_MODEL, D_MODEL), D_MODEL ** -0.5),
        "ffn2_pre_g": gain((DEPTH, D_MODEL)),
        "ffn2_post_g": gain((DEPTH, D_MODEL)),
        "ffn2_w_up": nrm((DEPTH, D_MODEL, 2 * D_FF), D_MODEL ** -0.5),
        "ffn2_w_down": nrm((DEPTH, D_FF, D_MODEL), D_FF ** -0.5),
    }


def reference(x_prompt, x_sample, cache_k, cache_v, state_ssm_re, state_ssm_im, page_table,
              ffn1_pre_g, ffn1_post_g, ffn1_w_up, ffn1_w_down,
              mix_pre_g, mix_post_g, w_in, sb_bias, lam_re, lam_im, log_dt, b_re, b_im, c_re, c_im, d_skip,
              w_glu, w_br_ssm, w_br_att, w_gate, b_gate, w_out,
              ffn2_pre_g, ffn2_post_g, ffn2_w_up, ffn2_w_down):
    xp, xs = x_prompt, x_sample
    bsz_p = xp.shape[0]
    bsz_s = xs.shape[0]
    kp_all, vp_all, hrp_all, hip_all = [], [], [], []
    ks_all, vs_all, hrs_all, his_all = [], [], [], []
    for l in range(DEPTH):
        p = {
            'mix_pre_g': mix_pre_g[l], 'mix_post_g': mix_post_g[l], 'w_in': w_in[l], 'sb_bias': sb_bias[l],
            'lam_re': lam_re[l], 'lam_im': lam_im[l], 'log_dt': log_dt[l],
            'b_re': b_re[l], 'b_im': b_im[l], 'c_re': c_re[l], 'c_im': c_im[l], 'd_skip': d_skip[l],
            'w_glu': w_glu[l], 'w_br_ssm': w_br_ssm[l], 'w_br_att': w_br_att[l],
            'w_gate': w_gate[l], 'b_gate': b_gate[l], 'w_out': w_out[l],
        }
        xp = _ffn_sublayer(xp, ffn1_pre_g[l], ffn1_post_g[l], ffn1_w_up[l], ffn1_w_down[l])
        xs = _ffn_sublayer(xs, ffn1_pre_g[l], ffn1_post_g[l], ffn1_w_up[l], ffn1_w_down[l])
        h0 = jnp.zeros((bsz_p, SSM_GROUPS, SSM_STATE), state_ssm_re.dtype)
        xp, kp, vp, hrp, hip = _token_mixer(xp, p, h0, h0, None, None)
        k_past = cache_k[l][page_table].reshape(bsz_s, -1, N_HEADS, HEAD_DIM)
        v_past = cache_v[l][page_table].reshape(bsz_s, -1, N_HEADS, HEAD_DIM)
        xs, ks_, vs_, hrs, his = _token_mixer(xs, p, state_ssm_re[l], state_ssm_im[l], k_past, v_past)
        xp = _ffn_sublayer(xp, ffn2_pre_g[l], ffn2_post_g[l], ffn2_w_up[l], ffn2_w_down[l])
        xs = _ffn_sublayer(xs, ffn2_pre_g[l], ffn2_post_g[l], ffn2_w_up[l], ffn2_w_down[l])
        kp_all.append(kp); vp_all.append(vp); hrp_all.append(hrp); hip_all.append(hip)
        ks_all.append(ks_); vs_all.append(vs_); hrs_all.append(hrs); his_all.append(his)
    new_k_prompt = jnp.stack(kp_all)
    new_v_prompt = jnp.stack(vp_all)
    ssm_re_prompt = jnp.stack(hrp_all)
    ssm_im_prompt = jnp.stack(hip_all)
    new_k_sample = jnp.stack(ks_all)
    new_v_sample = jnp.stack(vs_all)
    ssm_re_sample = jnp.stack(hrs_all)
    ssm_im_sample = jnp.stack(his_all)
    return (xp, xs, new_k_prompt, new_v_prompt, ssm_re_prompt, ssm_im_prompt,
            new_k_sample, new_v_sample, ssm_re_sample, ssm_im_sample)
```

```python
import functools
import math

import jax
import jax.numpy as jnp
from jax import lax
from jax.experimental import pallas as pl
from jax.experimental.pallas import tpu as pltpu

F32 = jnp.float32
BF16 = jnp.bfloat16
RMS_EPS = 1e-6

LANES = 128
VMEM_LIMIT = 56 * 1024 * 1024

S5_CHUNK = 16
ATT_TQ = 256
ATT_TK = 256
ROW_TILE = 512
DEC_PAGES_PER_STEP = 8


def _cparams(*sem):
    return pltpu.CompilerParams(dimension_semantics=sem, vmem_limit_bytes=VMEM_LIMIT)


def _full(shape):
    n = len(shape)
    return pl.BlockSpec(shape, lambda *_: (0,) * n, pipeline_mode=pl.Buffered(1))


def _rms(x, g):
    return x * lax.rsqrt(jnp.mean(x * x, axis=-1, keepdims=True) + RMS_EPS) * g


def _dot(a, b):
    return jnp.dot(a, b, preferred_element_type=F32)


def _dot3(a, b):
    ah = a.astype(BF16)
    al = (a - ah.astype(F32)).astype(BF16)
    bh = b.astype(BF16)
    bl = (b - bh.astype(F32)).astype(BF16)
    return _dot(ah, bh) + _dot(ah, bl) + _dot(al, bh)


def _ffn_body(x_ref, pre_ref, post_ref, wup_ref, wdn_ref, o_ref, *, d_ff, fc):
    x = x_ref[...]
    u = _rms(x, pre_ref[...]).astype(BF16)
    acc = None
    for c in range(d_ff // fc):
        a = _dot(u, wup_ref[:, c * fc:(c + 1) * fc])
        b = _dot(u, wup_ref[:, d_ff + c * fc:d_ff + (c + 1) * fc])
        h = (a * jax.nn.sigmoid(a) * b).astype(BF16)
        p = _dot(h, wdn_ref[c * fc:(c + 1) * fc, :])
        acc = p if acc is None else acc + p
    o_ref[...] = x + 0.5 * _rms(acc, post_ref[...])


def _ffn(x, pre_g, post_g, w_up, w_dn, tm):
    n, d = x.shape
    d_ff = w_dn.shape[0]
    fc = d_ff // 2 if (d_ff // 2) % LANES == 0 else d_ff
    row = pl.BlockSpec((tm, d), lambda i: (i, 0))
    return pl.pallas_call(
        functools.partial(_ffn_body, d_ff=d_ff, fc=fc),
        grid=(n // tm,),
        in_specs=[row, _full((1, d)), _full((1, d)), _full((d, 2 * d_ff)), _full((d_ff, d))],
        out_specs=row,
        out_shape=jax.ShapeDtypeStruct((n, d), F32),
        compiler_params=_cparams("parallel"),
        name="ffn",
    )(x, pre_g, post_g, w_up, w_dn)


def _mixin_body(x_ref, g_ref, win_ref, s_ref, q_ref, k_ref, v_ref, vb_ref, *, ws, wa, q_scale):
    u = _rms(x_ref[...], g_ref[...]).astype(BF16)
    s_ref[...] = _dot(u, win_ref[:, :ws])
    q_ref[...] = (_dot(u, win_ref[:, ws:ws + wa]) * q_scale).astype(BF16)
    k_ref[...] = _dot(u, win_ref[:, ws + wa:ws + 2 * wa])
    v = _dot(u, win_ref[:, ws + 2 * wa:])
    v_ref[...] = v
    vb_ref[...] = v.astype(BF16)


def _mixin(x, g, w_in, ws, wa, q_scale, tm):
    n, d = x.shape
    row = lambda w: pl.BlockSpec((tm, w), lambda i: (i, 0))
    return pl.pallas_call(
        functools.partial(_mixin_body, ws=ws, wa=wa, q_scale=q_scale),
        grid=(n // tm,),
        in_specs=[row(d), _full((1, d)), _full(w_in.shape)],
        out_specs=[row(ws), row(wa), row(wa), row(wa), row(wa)],
        out_shape=[jax.ShapeDtypeStruct((n, ws), F32), jax.ShapeDtypeStruct((n, wa), BF16),
                   jax.ShapeDtypeStruct((n, wa), F32), jax.ShapeDtypeStruct((n, wa), F32),
                   jax.ShapeDtypeStruct((n, wa), BF16)],
        compiler_params=_cparams("parallel"),
        name="mixer_in",
    )(x, g, w_in)


def _s5_discretize(lam_re, lam_im, log_dt, b_re, b_im):
    dt = jnp.exp(log_dt)[:, None]
    mag = jnp.exp(lam_re * dt)
    lbr, lbi = mag * jnp.cos(lam_im * dt), mag * jnp.sin(lam_im * dt)
    nr, ni = lbr - 1.0, lbi
    den = lam_re * lam_re + lam_im * lam_im
    fr = ((nr * lam_re + ni * lam_im) / den)[:, :, None]
    fi = ((ni * lam_re - nr * lam_im) / den)[:, :, None]
    return lbr, lbi, fr * b_re - fi * b_im, fr * b_im + fi * b_re


def _s5_chunk_mats(lbr, lbi, bbr, bbi, c_re, c_im, d_skip, n_l):
    g, p = lbr.shape
    c = bbr.shape[2]
    hi = lax.Precision.HIGHEST
    pr, pi = [jnp.ones_like(lbr)], [jnp.zeros_like(lbr)]
    for _ in range(n_l):
        pr, pi = pr + [pr[-1] * lbr - pi[-1] * lbi], pi + [pr[-1] * lbi + pi[-1] * lbr]
    pr, pi = jnp.stack(pr), jnp.stack(pi)
    clr = c_re[None] * pr[:, :, None, :] - c_im[None] * pi[:, :, None, :]
    cli = c_re[None] * pi[:, :, None, :] + c_im[None] * pr[:, :, None, :]
    gj = (jnp.einsum('jgcp,gpd->jgcd', clr[:n_l], bbr, precision=hi)
          - jnp.einsum('jgcp,gpd->jgcd', cli[:n_l], bbi, precision=hi))
    s_idx = jnp.arange(n_l)[:, None]
    t_idx = jnp.arange(n_l)[None, :]
    m = jnp.where((t_idx >= s_idx)[:, :, None, None, None], gj[jnp.maximum(t_idx - s_idx, 0)], 0.0)
    m = m.transpose(2, 0, 4, 1, 3).reshape(g, n_l * c, n_l * c)
    rev_r, rev_i = pr[n_l - 1::-1][:n_l], pi[n_l - 1::-1][:n_l]
    br = rev_r[:, :, :, None] * bbr[None] - rev_i[:, :, :, None] * bbi[None]
    bi = rev_r[:, :, :, None] * bbi[None] + rev_i[:, :, :, None] * bbr[None]
    br = br.transpose(1, 0, 3, 2).reshape(g, n_l * c, p)
    bi = bi.transpose(1, 0, 3, 2).reshape(g, n_l * c, p)
    cr = clr[1:].transpose(1, 3, 0, 2).reshape(g, p, n_l * c)
    ci = -cli[1:].transpose(1, 3, 0, 2).reshape(g, p, n_l * c)

    g2, lc = g // 2, n_l * c
    zero = lambda *s: jnp.zeros(s, F32)
    mp = zero(g2, 2, lc, 2, lc).at[:, 0, :, 0, :].set(m[0::2]).at[:, 1, :, 1, :].set(m[1::2])
    bp = zero(g2, 2, lc, 2, 2, p)
    cpr = zero(g2, 2, p, 2, lc)
    cpi = zero(g2, 2, p, 2, lc)
    for r in range(2):
        bp = bp.at[:, r, :, 0, r, :].set(br[r::2]).at[:, r, :, 1, r, :].set(bi[r::2])
        cpr = cpr.at[:, r, :, r, :].set(cr[r::2])
        cpi = cpi.at[:, r, :, r, :].set(ci[r::2])
    dv = jnp.broadcast_to(d_skip.reshape(g2, 2, 1, c), (g2, 2, n_l, c)).reshape(g2, 1, 2 * lc)
    return (mp.reshape(g2, 2 * lc, 2 * lc).astype(BF16), bp.reshape(g2, 2 * lc, 4 * p).astype(BF16),
            cpr.reshape(g2, 2 * p, 2 * lc).astype(BF16), cpi.reshape(g2, 2 * p, 2 * lc).astype(BF16),
            dv, pr[n_l].reshape(1, g * p), pi[n_l].reshape(1, g * p))


def _s5_local_body(u_ref, b_ref, sre_ref, sim_ref):
    s = _dot(u_ref[...].astype(BF16), b_ref[...])
    half = s.shape[1] // 2
    sre_ref[...] = s[:, :half]
    sim_ref[...] = s[:, half:]


def _s5_scan_body(sre_ref, sim_ref, ar_ref, ai_ref, h0r_ref, h0i_ref, hr_ref, hi_ref, fr_ref, fi_ref, *, nc):
    ar, ai = ar_ref[...], ai_ref[...]

    def step(k, carry):
        hr, hi = carry
        hr_ref[pl.ds(k, 1), :] = hr
        hi_ref[pl.ds(k, 1), :] = hi
        sr = sre_ref[pl.ds(k, 1), :]
        si = sim_ref[pl.ds(k, 1), :]
        return ar * hr - ai * hi + sr, ar * hi + ai * hr + si

    hr, hi = lax.fori_loop(0, nc, step, (h0r_ref[...], h0i_ref[...]), unroll=8)
    fr_ref[...] = hr
    fi_ref[...] = hi


def _s5_out_body(u_ref, m_ref, hr_ref, hi_ref, cr_ref, ci_ref, d_ref, y_ref):
    u = u_ref[...]
    y = _dot(u.astype(BF16), m_ref[...])
    y += _dot(hr_ref[...].astype(BF16), cr_ref[...])
    y += _dot(hi_ref[...].astype(BF16), ci_ref[...])
    y_ref[...] = y + d_ref[...] * u


def _s5_prompt(s_in, h0r, h0i, mats, n_l, g, c, p):
    t = s_in.shape[0]
    mp, bp, cpr, cpi, dv, alr, ali = mats
    g2, nc, lc2, p2 = g // 2, t // n_l, 2 * n_l * c, 2 * p
    u = s_in.reshape(nc, n_l, g2, 2, c).transpose(2, 0, 3, 1, 4).reshape(g2, nc, lc2)
    grp = lambda *s: pl.BlockSpec((None,) + s, lambda i: (i,) + (0,) * len(s))
    col = lambda r, w: pl.BlockSpec((r, w), lambda i: (0, i))
    state = jax.ShapeDtypeStruct((nc, g * p), F32)
    sre, sim = pl.pallas_call(
        _s5_local_body,
        grid=(g2,),
        in_specs=[grp(nc, lc2), grp(lc2, 2 * p2)],
        out_specs=[col(nc, p2), col(nc, p2)],
        out_shape=[state, state],
        compiler_params=_cparams("parallel"),
        name="s5_local",
    )(u, bp)
    sw = 4 * LANES
    last = jax.ShapeDtypeStruct((1, g * p), F32)
    hr, hi, fr, fi = pl.pallas_call(
        functools.partial(_s5_scan_body, nc=nc),
        grid=(g * p // sw,),
        in_specs=[col(nc, sw), col(nc, sw), col(1, sw), col(1, sw), col(1, sw), col(1, sw)],
        out_specs=[col(nc, sw), col(nc, sw), col(1, sw), col(1, sw)],
        out_shape=[state, state, last, last],
        compiler_params=_cparams("parallel"),
        name="s5_scan",
    )(sre, sim, alr, ali, h0r, h0i)
    y = pl.pallas_call(
        _s5_out_body,
        grid=(g2,),
        in_specs=[grp(nc, lc2), grp(lc2, lc2), col(nc, p2), col(nc, p2), grp(p2, lc2), grp(p2, lc2), grp(1, lc2)],
        out_specs=grp(nc, lc2),
        out_shape=jax.ShapeDtypeStruct((g2, nc, lc2), F32),
        compiler_params=_cparams("parallel"),
        name="s5_out",
    )(u, mp, hr, hi, cpr, cpi, dv)
    y = y.reshape(g2, nc, 2, n_l, c).transpose(1, 3, 0, 2, 4).reshape(t, g * c)
    return y, fr, fi


def _s5_step_body(u_ref, h0r_ref, h0i_ref, lr_ref, li_ref, br_ref, bi_ref, cr_ref, ci_ref, d_ref,
                  y_ref, hr_ref, hi_ref):
    u = u_ref[...]
    h0r, h0i, lr, li = h0r_ref[...], h0i_ref[...], lr_ref[...], li_ref[...]
    hr = lr * h0r - li * h0i + _dot3(u, br_ref[...])
    hi = lr * h0i + li * h0r + _dot3(u, bi_ref[...])
    hr_ref[...] = hr
    hi_ref[...] = hi
    y_ref[...] = _dot3(hr, cr_ref[...]) - _dot3(hi, ci_ref[...]) + d_ref[...] * u


def _s5_step(u, h0r, h0i, lbr, lbi, bbr, bbi, c_re, c_im, d_skip):
    bsz = u.shape[0]
    g, p, c = bbr.shape
    gb = LANES // c
    nb = g // gb
    eye = jnp.eye(gb, dtype=F32)

    def bd_in(b):
        return jnp.einsum('kapc,ab->kacbp', b.reshape(nb, gb, p, c), eye).reshape(nb, gb * c, gb * p)

    def bd_out(m):
        return jnp.einsum('kacp,ab->kapbc', m.reshape(nb, gb, c, p), eye).reshape(nb, gb * p, gb * c)

    blk = lambda r, w: pl.BlockSpec((r, w), lambda k: (0, k))
    grp = lambda r, w: pl.BlockSpec((None, r, w), lambda k: (k, 0, 0))
    wc, wp = gb * c, gb * p
    return pl.pallas_call(
        _s5_step_body,
        grid=(nb,),
        in_specs=[blk(bsz, wc), blk(bsz, wp), blk(bsz, wp), blk(1, wp), blk(1, wp),
                  grp(wc, wp), grp(wc, wp), grp(wp, wc), grp(wp, wc), blk(1, wc)],
        out_specs=[blk(bsz, wc), blk(bsz, wp), blk(bsz, wp)],
        out_shape=[jax.ShapeDtypeStruct((bsz, g * c), F32), jax.ShapeDtypeStruct((bsz, g * p), F32),
                   jax.ShapeDtypeStruct((bsz, g * p), F32)],
        compiler_params=_cparams("parallel"),
        name="s5_step",
    )(u, h0r, h0i, lbr.reshape(1, g * p), lbi.reshape(1, g * p), bd_in(bbr), bd_in(bbi),
      bd_out(c_re), bd_out(c_im), d_skip.reshape(1, g * c))


def _log_gates(z):
    lk = -(jnp.maximum(z, 0.0) + jnp.log(1.0 + jnp.exp(-jnp.abs(z))))
    return lk + z, lk


def _att_prompt_body(bias_ref, q_ref, kt_ref, v_ref, tri_ref, o_ref, *, tq, tk, dh):
    pair = pl.program_id(0)
    qi = pl.program_id(1)
    q = q_ref[...]
    lane = lax.broadcasted_iota(jnp.int32, q.shape, 1)
    row = lax.broadcasted_iota(jnp.int32, (tq, tk), 0)
    col = lax.broadcasted_iota(jnp.int32, (tq, tk), 1)
    heads = q.shape[1] // dh
    out = None
    for hh in range(heads):
        in_head = (lane >= hh * dh) & (lane < (hh + 1) * dh)
        qh = jnp.where(in_head, q, jnp.zeros_like(q))
        bias = bias_ref[pair * heads + hh]

        def block(j, carry, diagonal):
            run, acc = carry
            k0 = pl.multiple_of(j * tk, tk)
            z = _dot(qh, kt_ref[:, pl.ds(k0, tk)]) + bias
            lb, lk = _log_gates(z)
            if diagonal:
                valid = col < row
                lk = jnp.where(valid, lk, 0.0)
            after = _dot(lk.astype(BF16), tri_ref[...])
            w = jnp.exp(lb + after + run)
            if diagonal:
                w = jnp.where(valid, w, 0.0)
            acc = acc + _dot(w.astype(BF16), v_ref[pl.ds(k0, tk), :])
            run = run + jnp.sum(lk, axis=1, keepdims=True)
            return run, acc

        carry = (jnp.zeros((tq, 1), F32), jnp.zeros((tq, q.shape[1]), F32))
        carry = block(qi, carry, True)
        _, acc = lax.fori_loop(0, qi, lambda n, cr: block(qi - 1 - n, cr, False), carry)
        out = acc if out is None else jnp.where(in_head, acc, out)
    o_ref[...] = out.astype(o_ref.dtype)


def _att_prompt(q, kt, v, bias, dh, tq, tk):
    assert tq == tk
    t, w = q.shape
    tri = (jnp.arange(tk)[:, None] > jnp.arange(tk)[None, :]).astype(BF16)
    return pl.pallas_call(
        functools.partial(_att_prompt_body, tq=tq, tk=tk, dh=dh),
        grid=(w // LANES, t // tq),
        in_specs=[pl.BlockSpec(memory_space=pltpu.SMEM),
                  pl.BlockSpec((tq, LANES), lambda p, i: (i, p)),
                  pl.BlockSpec((LANES, t), lambda p, i: (p, 0)),
                  pl.BlockSpec((t, LANES), lambda p, i: (0, p)),
                  pl.BlockSpec((tk, tk), lambda p, i: (0, 0))],
        out_specs=pl.BlockSpec((tq, LANES), lambda p, i: (i, p)),
        out_shape=jax.ShapeDtypeStruct((t, w), BF16),
        compiler_params=_cparams("parallel", "parallel"),
        name="att_prompt",
    )(bias, q, kt, v, tri)


def _att_decode_body(pt_ref, q_ref, bias_ref, tri_ref, *rest, pps, dh):
    k_refs, v_refs = rest[:pps], rest[pps:2 * pps]
    o_ref, acc_ref, run_ref = rest[2 * pps:]
    step = pl.program_id(1)
    heads, w = acc_ref.shape

    @pl.when(step == 0)
    def _():
        acc_ref[...] = jnp.zeros_like(acc_ref)
        run_ref[...] = jnp.zeros_like(run_ref)

    head_of_lane = lax.broadcasted_iota(jnp.int32, (heads, w), 1) // dh
    own = head_of_lane == lax.broadcasted_iota(jnp.int32, (heads, w), 0)
    qbd = jnp.where(own, jnp.broadcast_to(q_ref[0].astype(F32), (heads, w)), 0.0).astype(BF16)
    bias = bias_ref[...]
    tri = tri_ref[...]
    acc = acc_ref[...]
    run = run_ref[...]
    for r in range(pps):
        kp = k_refs[r][0].astype(BF16)
        z = lax.dot_general(qbd, kp, (((1,), (1,)), ((), ())), preferred_element_type=F32) + bias
        lb, lk = _log_gates(z)
        lk_hi = lk.astype(BF16)
        lk_lo = (lk - lk_hi.astype(F32)).astype(BF16)
        after = _dot(lk_hi, tri) + _dot(lk_lo, tri)
        wgt = jnp.exp(lb + after + run)
        acc = acc + _dot(wgt.astype(BF16), v_refs[r][0].astype(BF16))
        run = run + jnp.sum(lk, axis=1, keepdims=True)
    acc_ref[...] = acc
    run_ref[...] = run

    @pl.when(step == pl.num_programs(1) - 1)
    def _():
        o_ref[0] = jnp.sum(jnp.where(own, acc, 0.0), axis=0, keepdims=True)


def _att_decode(q, cache_k, cache_v, page_table, page_base, bias, dh, pps):
    bsz, w = q.shape
    n_pages = page_table.shape[1]
    page = cache_k.shape[1]
    heads = w // dh
    assert n_pages % pps == 0
    pt = (page_table + page_base).reshape(-1)
    tri = (jnp.arange(page)[:, None] > jnp.arange(page)[None, :]).astype(BF16)
    bias_b = jnp.broadcast_to(bias[:, None], (heads, page))

    def page_spec(r):
        return pl.BlockSpec((1, page, w), lambda b, j, pt_ref: (pt_ref[b * n_pages + n_pages - 1 - (j * pps + r)], 0, 0))

    cache_specs = [page_spec(r) for r in range(pps)]
    out = pl.pallas_call(
        functools.partial(_att_decode_body, pps=pps, dh=dh),
        grid_spec=pltpu.PrefetchScalarGridSpec(
            num_scalar_prefetch=1,
            grid=(bsz, n_pages // pps),
            in_specs=[pl.BlockSpec((1, 1, w), lambda b, j, pt_ref: (b, 0, 0)),
                      pl.BlockSpec((heads, page), lambda b, j, pt_ref: (0, 0)),
                      pl.BlockSpec((page, page), lambda b, j, pt_ref: (0, 0))] + cache_specs + cache_specs,
            out_specs=pl.BlockSpec((1, 1, w), lambda b, j, pt_ref: (b, 0, 0)),
            scratch_shapes=[pltpu.VMEM((heads, w), F32), pltpu.VMEM((heads, 1), F32)],
        ),
        out_shape=jax.ShapeDtypeStruct((bsz, 1, w), F32),
        compiler_params=_cparams("parallel", "arbitrary"),
        name="att_decode",
    )(pt, q.reshape(bsz, 1, w), bias_b, tri, *([cache_k] * pps), *([cache_v] * pps))
    return out.reshape(bsz, w)


def _merge_body(x_ref, ys_ref, att_ref, pre_ref, post_ref, wglu_ref, wbs_ref, wba_ref, wg_ref, bg_ref, wo_ref,
                o_ref):
    x = x_ref[...]
    d = x.shape[1]
    ws = ys_ref.shape[1]
    u = _rms(x, pre_ref[...]).astype(BF16)
    glu = _dot(jax.nn.gelu(ys_ref[...]).astype(BF16), wglu_ref[...])
    y_s = (glu[:, :ws] * jax.nn.sigmoid(glu[:, ws:])).astype(BF16)
    o_s = _dot(y_s, wbs_ref[...])
    o_a = _dot(att_ref[...].astype(BF16), wba_ref[...])
    g_s = jax.nn.sigmoid(_dot(u, wg_ref[:, :d]) + bg_ref[:, :d])
    g_a = jax.nn.sigmoid(_dot(u, wg_ref[:, d:]) + bg_ref[:, d:])
    merged = (g_s * o_s + g_a * o_a).astype(BF16)
    o_ref[...] = x + _rms(_dot(merged, wo_ref[...]), post_ref[...])


def _merge(x, ys, att, pre_g, post_g, w_glu, w_bs, w_ba, w_gate, b_gate, w_out, tm):
    n, d = x.shape
    row = lambda a: pl.BlockSpec((tm, a.shape[1]), lambda i: (i, 0))
    weights = (pre_g, post_g, w_glu, w_bs, w_ba, w_gate, b_gate, w_out)
    return pl.pallas_call(
        _merge_body,
        grid=(n // tm,),
        in_specs=[row(x), row(ys), row(att)] + [_full(a.shape) for a in weights],
        out_specs=row(x),
        out_shape=jax.ShapeDtypeStruct((n, d), F32),
        compiler_params=_cparams("parallel"),
        name="merge",
    )(x, ys, att, *weights)


def kernel(x_prompt, x_sample, cache_k, cache_v, state_ssm_re, state_ssm_im, page_table, ffn1_pre_g, ffn1_post_g, ffn1_w_up, ffn1_w_down, mix_pre_g, mix_post_g, w_in, sb_bias, lam_re, lam_im, log_dt, b_re, b_im, c_re, c_im, d_skip, w_glu, w_br_ssm, w_br_att, w_gate, b_gate, w_out, ffn2_pre_g, ffn2_post_g, ffn2_w_up, ffn2_w_down):
    bsz_p, t, d = x_prompt.shape
    bsz_s, t_s, _ = x_sample.shape
    assert bsz_p == 1 and t_s == 1
    depth, n_phys, page, heads, dh = cache_k.shape
    _, g, p = lam_re.shape
    c = b_re.shape[3]
    ws, wa = g * c, heads * dh
    q_scale = 1.0 / math.sqrt(dh)
    tm = min(ROW_TILE, t)
    pps = math.gcd(DEC_PAGES_PER_STEP, page_table.shape[1])

    xp = x_prompt.reshape(t, d)
    xs = x_sample.reshape(bsz_s, d)
    pool_k = cache_k.reshape(depth * n_phys, page, wa)
    pool_v = cache_v.reshape(depth * n_phys, page, wa)
    row = lambda a: a.reshape(1, -1)
    zeros_state = jnp.zeros((1, g * p), F32)

    outs = [[] for _ in range(8)]
    for l in range(depth):
        wb = lambda a: a[l].astype(BF16)
        f1 = (row(ffn1_pre_g[l]), row(ffn1_post_g[l]), wb(ffn1_w_up), wb(ffn1_w_down))
        f2 = (row(ffn2_pre_g[l]), row(ffn2_post_g[l]), wb(ffn2_w_up), wb(ffn2_w_down))
        w_in_l = wb(w_in)
        mrg = (row(mix_pre_g[l]), row(mix_post_g[l]), wb(w_glu), wb(w_br_ssm), wb(w_br_att), wb(w_gate),
               row(b_gate[l]), wb(w_out))
        lbr, lbi, bbr, bbi = _s5_discretize(lam_re[l], lam_im[l], log_dt[l], b_re[l], b_im[l])
        mats = _s5_chunk_mats(lbr, lbi, bbr, bbi, c_re[l], c_im[l], d_skip[l], S5_CHUNK)

        xp = _ffn(xp, *f1, tm)
        s_in, q, k, v, vb = _mixin(xp, row(mix_pre_g[l]), w_in_l, ws, wa, q_scale, tm)
        y_s, hr, hi = _s5_prompt(s_in, zeros_state, zeros_state, mats, S5_CHUNK, g, c, p)
        att = _att_prompt(q, k.T.astype(BF16), vb, sb_bias[l], dh, min(ATT_TQ, t), min(ATT_TK, t))
        xp = _merge(xp, y_s, att, *mrg, tm)
        xp = _ffn(xp, *f2, tm)
        outs[0].append(k.reshape(bsz_p, t, heads, dh))
        outs[1].append(v.reshape(bsz_p, t, heads, dh))
        outs[2].append(hr.reshape(bsz_p, g, p))
        outs[3].append(hi.reshape(bsz_p, g, p))

        xs = _ffn(xs, *f1, bsz_s)
        s_in, q, k, v, _ = _mixin(xs, row(mix_pre_g[l]), w_in_l, ws, wa, q_scale, bsz_s)
        y_s, hr, hi = _s5_step(s_in, state_ssm_re[l].reshape(bsz_s, g * p), state_ssm_im[l].reshape(bsz_s, g * p),
                               lbr, lbi, bbr, bbi, c_re[l], c_im[l], d_skip[l])
        att = _att_decode(q, pool_k, pool_v, page_table, l * n_phys, sb_bias[l], dh, pps)
        xs = _merge(xs, y_s, att, *mrg, bsz_s)
        xs = _ffn(xs, *f2, bsz_s)
        outs[4].append(k.reshape(bsz_s, t_s, heads, dh))
        outs[5].append(v.reshape(bsz_s, t_s, heads, dh))
        outs[6].append(hr.reshape(bsz_s, g, p))
        outs[7].append(hi.reshape(bsz_s, g, p))

    st = [jnp.stack(o) for o in outs]
    return (xp.reshape(bsz_p, t, d), xs.reshape(bsz_s, t_s, d), st[0], st[1], st[2], st[3], st[4], st[5], st[6], st[7])
```

```python
import functools
import math

import jax
import jax.numpy as jnp
from jax import lax
from jax.experimental import pallas as pl
from jax.experimental.pallas import tpu as pltpu

F32 = jnp.float32
BF16 = jnp.bfloat16
RMS_EPS = 1e-6
LOG2_E = 1.4426950408889634
EXP2_CLAMP = 126.0
MASKED_LOG2 = -1e30

LANES = 128
VMEM_LIMIT = 56 * 1024 * 1024

S5_CHUNK = 16
S5_ROWS = 256
ATT_TQ = 256
ATT_TK = 256
ROW_TILE = 512
DEC_PAGES_PER_STEP = 8


def _cparams(*sem):
    return pltpu.CompilerParams(dimension_semantics=sem, vmem_limit_bytes=VMEM_LIMIT)


def _full(shape):
    n = len(shape)
    return pl.BlockSpec(shape, lambda *_: (0,) * n, pipeline_mode=pl.Buffered(1))


def _rms(x, g):
    return x * lax.rsqrt(jnp.mean(x * x, axis=-1, keepdims=True) + RMS_EPS) * g


def _dot(a, b):
    return jnp.dot(a, b, preferred_element_type=F32)


def _dot3(a, b):
    ah = a.astype(BF16)
    al = (a - ah.astype(F32)).astype(BF16)
    bh = b.astype(BF16)
    bl = (b - bh.astype(F32)).astype(BF16)
    return _dot(ah, bh) + _dot(ah, bl) + _dot(al, bh)


def _ffn_body(x_ref, pre_ref, post_ref, wup_ref, wdn_ref, o_ref, *, d_ff, fc):
    x = x_ref[...]
    u = _rms(x, pre_ref[...]).astype(BF16)
    acc = None
    for c in range(d_ff // fc):
        a = _dot(u, wup_ref[:, c * fc:(c + 1) * fc])
        b = _dot(u, wup_ref[:, d_ff + c * fc:d_ff + (c + 1) * fc])
        h = (a * jax.nn.sigmoid(a) * b).astype(BF16)
        p = _dot(h, wdn_ref[c * fc:(c + 1) * fc, :])
        acc = p if acc is None else acc + p
    o_ref[...] = x + 0.5 * _rms(acc, post_ref[...])


def _ffn(x, pre_g, post_g, w_up, w_dn, tm):
    n, d = x.shape
    d_ff = w_dn.shape[0]
    fc = d_ff // 2 if (d_ff // 2) % LANES == 0 else d_ff
    row = pl.BlockSpec((tm, d), lambda i: (i, 0))
    return pl.pallas_call(
        functools.partial(_ffn_body, d_ff=d_ff, fc=fc),
        grid=(n // tm,),
        in_specs=[row, _full((1, d)), _full((1, d)), _full((d, 2 * d_ff)), _full((d_ff, d))],
        out_specs=row,
        out_shape=jax.ShapeDtypeStruct((n, d), F32),
        compiler_params=_cparams("parallel"),
        name="ffn",
    )(x, pre_g, post_g, w_up, w_dn)


def _mixin_body(x_ref, g_ref, win_ref, s_ref, q_ref, k_ref, v_ref, vb_ref, *, ws, wa, q_scale):
    u = _rms(x_ref[...], g_ref[...]).astype(BF16)
    s_ref[...] = _dot(u, win_ref[:, :ws])
    q_ref[...] = (_dot(u, win_ref[:, ws:ws + wa]) * q_scale).astype(BF16)
    k_ref[...] = _dot(u, win_ref[:, ws + wa:ws + 2 * wa])
    v = _dot(u, win_ref[:, ws + 2 * wa:])
    v_ref[...] = v
    vb_ref[...] = v.astype(BF16)


def _mixin(x, g, w_in, ws, wa, q_scale, tm):
    n, d = x.shape
    row = lambda w: pl.BlockSpec((tm, w), lambda i: (i, 0))
    return pl.pallas_call(
        functools.partial(_mixin_body, ws=ws, wa=wa, q_scale=q_scale),
        grid=(n // tm,),
        in_specs=[row(d), _full((1, d)), _full(w_in.shape)],
        out_specs=[row(ws), row(wa), row(wa), row(wa), row(wa)],
        out_shape=[jax.ShapeDtypeStruct((n, ws), F32), jax.ShapeDtypeStruct((n, wa), BF16),
                   jax.ShapeDtypeStruct((n, wa), F32), jax.ShapeDtypeStruct((n, wa), F32),
                   jax.ShapeDtypeStruct((n, wa), BF16)],
        compiler_params=_cparams("parallel"),
        name="mixer_in",
    )(x, g, w_in)


def _s5_discretize(lam_re, lam_im, log_dt, b_re, b_im):
    dt = jnp.exp(log_dt)[:, None]
    mag = jnp.exp(lam_re * dt)
    lbr, lbi = mag * jnp.cos(lam_im * dt), mag * jnp.sin(lam_im * dt)
    nr, ni = lbr - 1.0, lbi
    den = lam_re * lam_re + lam_im * lam_im
    fr = ((nr * lam_re + ni * lam_im) / den)[:, :, None]
    fi = ((ni * lam_re - nr * lam_im) / den)[:, :, None]
    return lbr, lbi, fr * b_re - fi * b_im, fr * b_im + fi * b_re


def _s5_chunk_mats(lbr, lbi, bbr, bbi, c_re, c_im, n_l):
    g, p = lbr.shape
    c = bbr.shape[2]
    gb = LANES // c
    nb = g // gb
    eye = jnp.eye(gb, dtype=F32)
    hi = lax.Precision.HIGHEST
    pr, pi = [jnp.ones_like(lbr)], [jnp.zeros_like(lbr)]
    for _ in range(n_l):
        pr, pi = pr + [pr[-1] * lbr - pi[-1] * lbi], pi + [pr[-1] * lbi + pi[-1] * lbr]
    rev_r, rev_i = jnp.stack(pr[n_l - 1::-1]), jnp.stack(pi[n_l - 1::-1])
    pr, pi = jnp.stack(pr), jnp.stack(pi)
    clr = c_re[None] * pr[:, :, None, :] - c_im[None] * pi[:, :, None, :]
    cli = c_re[None] * pi[:, :, None, :] + c_im[None] * pr[:, :, None, :]
    gj = (jnp.einsum('jgcp,gpd->jgcd', clr[:n_l], bbr, precision=hi)
          - jnp.einsum('jgcp,gpd->jgcd', cli[:n_l], bbi, precision=hi))
    s_idx = jnp.arange(n_l)[:, None]
    t_idx = jnp.arange(n_l)[None, :]
    gm = jnp.where((t_idx >= s_idx)[:, :, None, None, None], gj[jnp.maximum(t_idx - s_idx, 0)], 0.0)
    gm = gm.reshape(n_l, n_l, nb, gb, c, c).transpose(2, 0, 3, 5, 1, 4)
    m = gm[:, :, :, :, :, None, :] * eye[None, None, :, None, None, :, None]
    m = m.reshape(nb, n_l * LANES, n_l * LANES).astype(BF16)
    br = rev_r[:, :, :, None] * bbr[None] - rev_i[:, :, :, None] * bbi[None]
    bi = rev_r[:, :, :, None] * bbi[None] + rev_i[:, :, :, None] * bbr[None]

    def bd_b(b):
        b = b.reshape(n_l, nb, gb, p, c).transpose(1, 0, 2, 4, 3)
        b = b[:, :, :, :, None, :] * eye[None, None, :, None, :, None]
        return b.reshape(nb, n_l * LANES, gb * p)

    bm = jnp.concatenate([bd_b(br), bd_b(bi)], axis=-1).astype(BF16)

    def bd_c(m_):
        m_ = m_.reshape(n_l, nb, gb, c, p).transpose(1, 2, 4, 0, 3)
        m_ = m_[:, :, :, :, None, :] * eye[None, :, None, None, :, None]
        return m_.reshape(nb, gb * p, n_l * LANES).astype(BF16)

    return m, bm, bd_c(clr[1:]), bd_c(-cli[1:]), pr[n_l].reshape(1, g * p), pi[n_l].reshape(1, g * p)


def _gather_steps(u_ref, ucat_ref, n_l):
    for s in range(n_l):
        ucat_ref[:, s * LANES:(s + 1) * LANES] = u_ref[:, s, :].astype(BF16)


def _s5_local_body(u_ref, b_ref, sre_ref, sim_ref, ucat_ref, *, n_l):
    _gather_steps(u_ref, ucat_ref, n_l)
    st = _dot(ucat_ref[...], b_ref[...])
    half = st.shape[1] // 2
    sre_ref[...] = st[:, :half]
    sim_ref[...] = st[:, half:]


def _s5_scan_body(sre_ref, sim_ref, ar_ref, ai_ref, h0r_ref, h0i_ref, hr_ref, hi_ref, fr_ref, fi_ref, *, nc):
    ar, ai = ar_ref[...], ai_ref[...]

    def step(k, carry):
        hr, hi = carry
        hr_ref[pl.ds(k, 1), :] = hr
        hi_ref[pl.ds(k, 1), :] = hi
        sr = sre_ref[pl.ds(k, 1), :]
        si = sim_ref[pl.ds(k, 1), :]
        return ar * hr - ai * hi + sr, ar * hi + ai * hr + si

    hr, hi = lax.fori_loop(0, nc, step, (h0r_ref[...], h0i_ref[...]), unroll=8)
    fr_ref[...] = hr
    fi_ref[...] = hi


def _s5_out_body(u_ref, m_ref, hr_ref, hi_ref, cr_ref, ci_ref, d_ref, y_ref, ucat_ref, *, n_l):
    _gather_steps(u_ref, ucat_ref, n_l)
    y = _dot(ucat_ref[...], m_ref[...])
    y += _dot(hr_ref[...].astype(BF16), cr_ref[...])
    y += _dot(hi_ref[...].astype(BF16), ci_ref[...])
    d = d_ref[...]
    for t in range(n_l):
        y_ref[:, t, :] = y[:, t * LANES:(t + 1) * LANES] + d * u_ref[:, t, :]


def _s5_prompt(s_in, h0r, h0i, mats, d_skip, n_l, g, p):
    t, ws = s_in.shape
    m, bm, cr, ci, alr, ali = mats
    nb, nc = ws // LANES, t // n_l
    nr = min(S5_ROWS, nc)
    wp = g * p // nb
    u = s_in.reshape(nc, n_l, ws)
    u_spec = pl.BlockSpec((nr, n_l, LANES), lambda j, r: (r, 0, j))
    st_spec = pl.BlockSpec((nr, wp), lambda j, r: (r, j))
    mat = lambda a: pl.BlockSpec((None,) + a.shape[1:], lambda j, r: (j, 0, 0))
    ucat = pltpu.VMEM((nr, n_l * LANES), BF16)
    state = jax.ShapeDtypeStruct((nc, g * p), F32)
    sre, sim = pl.pallas_call(
        functools.partial(_s5_local_body, n_l=n_l),
        grid=(nb, nc // nr),
        in_specs=[u_spec, mat(bm)],
        out_specs=[st_spec, st_spec],
        out_shape=[state, state],
        scratch_shapes=[ucat],
        compiler_params=_cparams("parallel", "parallel"),
        name="s5_local",
    )(u, bm)
    sw = 4 * LANES
    col = lambda r, w: pl.BlockSpec((r, w), lambda i: (0, i))
    last = jax.ShapeDtypeStruct((1, g * p), F32)
    hr, hi, fr, fi = pl.pallas_call(
        functools.partial(_s5_scan_body, nc=nc),
        grid=(g * p // sw,),
        in_specs=[col(nc, sw), col(nc, sw), col(1, sw), col(1, sw), col(1, sw), col(1, sw)],
        out_specs=[col(nc, sw), col(nc, sw), col(1, sw), col(1, sw)],
        out_shape=[state, state, last, last],
        compiler_params=_cparams("parallel"),
        name="s5_scan",
    )(sre, sim, alr, ali, h0r, h0i)
    y = pl.pallas_call(
        functools.partial(_s5_out_body, n_l=n_l),
        grid=(nb, nc // nr),
        in_specs=[u_spec, mat(m), st_spec, st_spec, mat(cr), mat(ci),
                  pl.BlockSpec((1, LANES), lambda j, r: (0, j))],
        out_specs=u_spec,
        out_shape=jax.ShapeDtypeStruct((nc, n_l, ws), F32),
        scratch_shapes=[ucat],
        compiler_params=_cparams("parallel", "parallel"),
        name="s5_out",
    )(u, m, hr, hi, cr, ci, d_skip.reshape(1, ws))
    return y.reshape(t, ws), fr, fi


def _s5_step_body(u_ref, h0r_ref, h0i_ref, lr_ref, li_ref, br_ref, bi_ref, cr_ref, ci_ref, d_ref,
                  y_ref, hr_ref, hi_ref):
    u = u_ref[...]
    h0r, h0i, lr, li = h0r_ref[...], h0i_ref[...], lr_ref[...], li_ref[...]
    hr = lr * h0r - li * h0i + _dot3(u, br_ref[...])
    hi = lr * h0i + li * h0r + _dot3(u, bi_ref[...])
    hr_ref[...] = hr
    hi_ref[...] = hi
    y_ref[...] = _dot3(hr, cr_ref[...]) - _dot3(hi, ci_ref[...]) + d_ref[...] * u


def _s5_step(u, h0r, h0i, lbr, lbi, bbr, bbi, c_re, c_im, d_skip):
    bsz = u.shape[0]
    g, p, c = bbr.shape
    gb = LANES // c
    nb = g // gb
    eye = jnp.eye(gb, dtype=F32)

    def bd_in(b):
        return jnp.einsum('kapc,ab->kacbp', b.reshape(nb, gb, p, c), eye).reshape(nb, gb * c, gb * p)

    def bd_out(m):
        return jnp.einsum('kacp,ab->kapbc', m.reshape(nb, gb, c, p), eye).reshape(nb, gb * p, gb * c)

    blk = lambda r, w: pl.BlockSpec((r, w), lambda k: (0, k))
    grp = lambda r, w: pl.BlockSpec((None, r, w), lambda k: (k, 0, 0))
    wc, wp = gb * c, gb * p
    return pl.pallas_call(
        _s5_step_body,
        grid=(nb,),
        in_specs=[blk(bsz, wc), blk(bsz, wp), blk(bsz, wp), blk(1, wp), blk(1, wp),
                  grp(wc, wp), grp(wc, wp), grp(wp, wc), grp(wp, wc), blk(1, wc)],
        out_specs=[blk(bsz, wc), blk(bsz, wp), blk(bsz, wp)],
        out_shape=[jax.ShapeDtypeStruct((bsz, g * c), F32), jax.ShapeDtypeStruct((bsz, g * p), F32),
                   jax.ShapeDtypeStruct((bsz, g * p), F32)],
        compiler_params=_cparams("parallel"),
        name="s5_step",
    )(u, h0r, h0i, lbr.reshape(1, g * p), lbi.reshape(1, g * p), bd_in(bbr), bd_in(bbi),
      bd_out(c_re), bd_out(c_im), d_skip.reshape(1, g * c))


def _softplus2(z):
    return jnp.maximum(jnp.log(1.0 + jnp.exp2(jnp.minimum(z, EXP2_CLAMP))) * LOG2_E, z)


def _att_prompt_body(bias_ref, q_ref, kt_ref, v_ref, tri_ref, o_ref,
                     s0, s1, lb0, lb1, sp0, sp1, rs0, rs1, la0, la1, w0, w1, acc_ref, run_ref, *, tq, tk, dh):
    s_raw, lbs, sps, rss, las, wbs = (s0, s1), (lb0, lb1), (sp0, sp1), (rs0, rs1), (la0, la1), (w0, w1)
    pair = pl.program_id(0)
    qi = pl.program_id(1)
    q = q_ref[...]
    heads = q.shape[1] // dh
    lane = lax.broadcasted_iota(jnp.int32, q.shape, 1)
    in_head = [(lane >= hh * dh) & (lane < (hh + 1) * dh) for hh in range(heads)]
    q2 = jnp.concatenate([jnp.where(m, q, jnp.zeros_like(q)) for m in in_head], axis=0)
    bias = [bias_ref[pair * heads + hh] * LOG2_E for hh in range(heads)]

    def keys_of(b):
        j = jnp.clip(qi - b, 0, qi)
        return pl.ds(pl.multiple_of(j * tk, tk), tk)

    def stage2(slot, penalty, diagonal):
        for hh in range(heads):
            r = slice(hh * tq, (hh + 1) * tq)
            z = s_raw[slot][r, :] + (bias[hh] + penalty)
            sp = _softplus2(z)
            lb = z - sp
            if diagonal:
                valid = lax.broadcasted_iota(jnp.int32, z.shape, 1) < lax.broadcasted_iota(jnp.int32, z.shape, 0)
                sp = jnp.where(valid, sp, 0.0)
                lb = jnp.where(valid, lb, MASKED_LOG2)
            lbs[slot][r, :] = lb
            sps[slot][r, :] = sp.astype(BF16)
            rss[slot][r, :] = jnp.sum(sp, axis=1, keepdims=True)

    def stage4(slot):
        wbs[slot][...] = jnp.exp2(lbs[slot][...] - las[slot][...] - run_ref[...]).astype(BF16)
        run_ref[...] += rss[slot][...]

    def trip(t, slot):
        other = 1 - slot
        las[slot][...] = _dot(sps[slot][...], tri_ref[...])
        s_raw[slot][...] = _dot(q2, kt_ref[:, keys_of(t)])
        acc_ref[...] += _dot(wbs[slot][...], v_ref[keys_of(t - 4), :])
        stage4(other)
        stage2(other, jnp.where(t - 1 > qi, MASKED_LOG2, 0.0), False)

    acc_ref[...] = jnp.zeros_like(acc_ref)
    run_ref[...] = jnp.zeros_like(run_ref)
    lb1[...] = jnp.full(lb1.shape, MASKED_LOG2, F32)
    rs1[...] = jnp.zeros_like(rs1)
    la1[...] = jnp.zeros_like(la1)
    w0[...] = jnp.zeros_like(w0)
    s0[...] = _dot(q2, kt_ref[:, keys_of(0)])
    stage2(0, 0.0, True)
    s1[...] = _dot(q2, kt_ref[:, keys_of(1)])

    def two_trips(i, _):
        trip(2 * i + 2, 0)
        trip(2 * i + 3, 1)
        return 0

    lax.fori_loop(0, (qi + 4) // 2, two_trips, 0)
    acc = acc_ref[...]
    out = acc[:tq]
    for hh in range(1, heads):
        out = jnp.where(in_head[hh], acc[hh * tq:(hh + 1) * tq], out)
    o_ref[...] = out.astype(o_ref.dtype)


def _att_prompt(q, kt, v, bias, dh, tq, tk):
    assert tq == tk
    t, w = q.shape
    rows = (LANES // dh) * tq
    tri = (jnp.arange(tk)[:, None] > jnp.arange(tk)[None, :]).astype(BF16)
    ring = lambda shape, dtype: [pltpu.VMEM(shape, dtype)] * 2
    return pl.pallas_call(
        functools.partial(_att_prompt_body, tq=tq, tk=tk, dh=dh),
        grid=(w // LANES, t // tq),
        in_specs=[pl.BlockSpec(memory_space=pltpu.SMEM),
                  pl.BlockSpec((tq, LANES), lambda p, i: (i, p)),
                  pl.BlockSpec((LANES, t), lambda p, i: (p, 0)),
                  pl.BlockSpec((t, LANES), lambda p, i: (0, p)),
                  _full((tk, tk))],
        out_specs=pl.BlockSpec((tq, LANES), lambda p, i: (i, p)),
        out_shape=jax.ShapeDtypeStruct((t, w), BF16),
        scratch_shapes=(ring((rows, tk), F32) + ring((rows, tk), F32) + ring((rows, tk), BF16)
                        + ring((rows, 1), F32) + ring((rows, tk), F32) + ring((rows, tk), BF16)
                        + [pltpu.VMEM((rows, LANES), F32), pltpu.VMEM((rows, 1), F32)]),
        compiler_params=_cparams("parallel", "parallel"),
        name="att_prompt",
    )(bias, q, kt, v, tri)


def _att_decode_body(pt_ref, q_ref, bias_ref, tri_ref, *rest, pps, dh):
    kt_refs, vt_refs = rest[:pps], rest[pps:2 * pps]
    o_ref, acc_ref, run_ref = rest[2 * pps:]
    step = pl.program_id(1)
    w, page = acc_ref.shape
    heads = w // dh

    @pl.when(step == 0)
    def _():
        acc_ref[...] = jnp.zeros_like(acc_ref)
        run_ref[...] = jnp.zeros_like(run_ref)

    head_of_lane = lax.broadcasted_iota(jnp.int32, (heads, w), 1) // dh
    own = head_of_lane == lax.broadcasted_iota(jnp.int32, (heads, w), 0)
    qbd = jnp.where(own, jnp.broadcast_to(q_ref[0].astype(F32), (heads, w)), 0.0).astype(BF16)
    bias = bias_ref[...]
    tri = tri_ref[...]
    run = run_ref[...]
    wgt = []
    for r in range(pps):
        z = _dot(qbd, kt_refs[r][0].astype(BF16)) + bias
        sp = _softplus2(z)
        sp_hi = sp.astype(BF16)
        sp_lo = (sp - sp_hi.astype(F32)).astype(BF16)
        later = _dot(sp_hi, tri) + _dot(sp_lo, tri)
        wgt.append(jnp.exp2(z - sp - later - run))
        run = run + jnp.sum(sp, axis=1, keepdims=True)
    run_ref[...] = run
    for hh in range(heads):
        rows = slice(hh * dh, (hh + 1) * dh)
        a = acc_ref[rows, :]
        for r in range(pps):
            a = a + vt_refs[r][0, rows, :] * jnp.broadcast_to(wgt[r][hh:hh + 1, :], (dh, page))
        acc_ref[rows, :] = a

    @pl.when(step == pl.num_programs(1) - 1)
    def _():
        o_ref[0] = jnp.sum(acc_ref[...], axis=1, keepdims=True)


def _att_decode(q, pool_kt, pool_vt, page_table, page_base, bias, dh, pps):
    bsz, w = q.shape
    n_pages = page_table.shape[1]
    page = pool_kt.shape[2]
    heads = w // dh
    assert n_pages % pps == 0
    pt = (page_table + page_base).reshape(-1)
    tri = (jnp.arange(page)[:, None] > jnp.arange(page)[None, :]).astype(BF16)
    bias_b = jnp.broadcast_to((bias * LOG2_E)[:, None], (heads, page))

    def page_spec(r):
        return pl.BlockSpec((1, w, page), lambda b, j, pt_ref: (pt_ref[b * n_pages + n_pages - 1 - (j * pps + r)], 0, 0))

    cache_specs = [page_spec(r) for r in range(pps)]
    out = pl.pallas_call(
        functools.partial(_att_decode_body, pps=pps, dh=dh),
        grid_spec=pltpu.PrefetchScalarGridSpec(
            num_scalar_prefetch=1,
            grid=(bsz, n_pages // pps),
            in_specs=[pl.BlockSpec((1, 1, w), lambda b, j, pt_ref: (b, 0, 0)),
                      pl.BlockSpec((heads, page), lambda b, j, pt_ref: (0, 0)),
                      pl.BlockSpec((page, page), lambda b, j, pt_ref: (0, 0))] + cache_specs + cache_specs,
            out_specs=pl.BlockSpec((1, w, 1), lambda b, j, pt_ref: (b, 0, 0)),
            scratch_shapes=[pltpu.VMEM((w, page), F32), pltpu.VMEM((heads, 1), F32)],
        ),
        out_shape=jax.ShapeDtypeStruct((bsz, w, 1), F32),
        compiler_params=_cparams("parallel", "arbitrary"),
        name="att_decode",
    )(pt, q.reshape(bsz, 1, w), bias_b, tri, *([pool_kt] * pps), *([pool_vt] * pps))
    return out.reshape(bsz, w)


def _merge_body(x_ref, ys_ref, att_ref, pre_ref, post_ref, wglu_ref, wbs_ref, wba_ref, wg_ref, bg_ref, wo_ref,
                o_ref):
    x = x_ref[...]
    d = x.shape[1]
    ws = ys_ref.shape[1]
    u = _rms(x, pre_ref[...]).astype(BF16)
    glu = _dot(jax.nn.gelu(ys_ref[...]).astype(BF16), wglu_ref[...])
    y_s = (glu[:, :ws] * jax.nn.sigmoid(glu[:, ws:])).astype(BF16)
    o_s = _dot(y_s, wbs_ref[...])
    o_a = _dot(att_ref[...].astype(BF16), wba_ref[...])
    g_s = jax.nn.sigmoid(_dot(u, wg_ref[:, :d]) + bg_ref[:, :d])
    g_a = jax.nn.sigmoid(_dot(u, wg_ref[:, d:]) + bg_ref[:, d:])
    merged = (g_s * o_s + g_a * o_a).astype(BF16)
    o_ref[...] = x + _rms(_dot(merged, wo_ref[...]), post_ref[...])


def _merge(x, ys, att, pre_g, post_g, w_glu, w_bs, w_ba, w_gate, b_gate, w_out, tm):
    n, d = x.shape
    row = lambda a: pl.BlockSpec((tm, a.shape[1]), lambda i: (i, 0))
    weights = (pre_g, post_g, w_glu, w_bs, w_ba, w_gate, b_gate, w_out)
    return pl.pallas_call(
        _merge_body,
        grid=(n // tm,),
        in_specs=[row(x), row(ys), row(att)] + [_full(a.shape) for a in weights],
        out_specs=row(x),
        out_shape=jax.ShapeDtypeStruct((n, d), F32),
        compiler_params=_cparams("parallel"),
        name="merge",
    )(x, ys, att, *weights)


def kernel(x_prompt, x_sample, cache_k, cache_v, state_ssm_re, state_ssm_im, page_table, ffn1_pre_g, ffn1_post_g, ffn1_w_up, ffn1_w_down, mix_pre_g, mix_post_g, w_in, sb_bias, lam_re, lam_im, log_dt, b_re, b_im, c_re, c_im, d_skip, w_glu, w_br_ssm, w_br_att, w_gate, b_gate, w_out, ffn2_pre_g, ffn2_post_g, ffn2_w_up, ffn2_w_down):
    bsz_p, t, d = x_prompt.shape
    bsz_s, t_s, _ = x_sample.shape
    assert bsz_p == 1 and t_s == 1
    depth, n_phys, page, heads, dh = cache_k.shape
    _, g, p = lam_re.shape
    c = b_re.shape[3]
    ws, wa = g * c, heads * dh
    q_scale = LOG2_E / math.sqrt(dh)
    tm = min(ROW_TILE, t)
    pps = math.gcd(DEC_PAGES_PER_STEP, page_table.shape[1])

    xp = x_prompt.reshape(t, d)
    xs = x_sample.reshape(bsz_s, d)
    pool_kt = cache_k.transpose(0, 1, 3, 4, 2).reshape(depth * n_phys, wa, page)
    pool_vt = cache_v.transpose(0, 1, 3, 4, 2).reshape(depth * n_phys, wa, page)
    row = lambda a: a.reshape(1, -1)
    zeros_state = jnp.zeros((1, g * p), F32)

    outs = [[] for _ in range(8)]
    for l in range(depth):
        wb = lambda a: a[l].astype(BF16)
        f1 = (row(ffn1_pre_g[l]), row(ffn1_post_g[l]), wb(ffn1_w_up), wb(ffn1_w_down))
        f2 = (row(ffn2_pre_g[l]), row(ffn2_post_g[l]), wb(ffn2_w_up), wb(ffn2_w_down))
        w_in_l = wb(w_in)
        mrg = (row(mix_pre_g[l]), row(mix_post_g[l]), wb(w_glu), wb(w_br_ssm), wb(w_br_att), wb(w_gate),
               row(b_gate[l]), wb(w_out))
        lbr, lbi, bbr, bbi = _s5_discretize(lam_re[l], lam_im[l], log_dt[l], b_re[l], b_im[l])
        mats = _s5_chunk_mats(lbr, lbi, bbr, bbi, c_re[l], c_im[l], S5_CHUNK)

        xp = _ffn(xp, *f1, tm)
        s_in, q, k, v, vb = _mixin(xp, row(mix_pre_g[l]), w_in_l, ws, wa, q_scale, tm)
        y_s, hr, hi = _s5_prompt(s_in, zeros_state, zeros_state, mats, d_skip[l], S5_CHUNK, g, p)
        att = _att_prompt(q, k.T.astype(BF16), vb, sb_bias[l], dh, min(ATT_TQ, t), min(ATT_TK, t))
        xp = _merge(xp, y_s, att, *mrg, tm)
        xp = _ffn(xp, *f2, tm)
        outs[0].append(k.reshape(bsz_p, t, heads, dh))
        outs[1].append(v.reshape(bsz_p, t, heads, dh))
        outs[2].append(hr.reshape(bsz_p, g, p))
        outs[3].append(hi.reshape(bsz_p, g, p))

        xs = _ffn(xs, *f1, bsz_s)
        s_in, q, k, v, _ = _mixin(xs, row(mix_pre_g[l]), w_in_l, ws, wa, q_scale, bsz_s)
        y_s, hr, hi = _s5_step(s_in, state_ssm_re[l].reshape(bsz_s, g * p), state_ssm_im[l].reshape(bsz_s, g * p),
                               lbr, lbi, bbr, bbi, c_re[l], c_im[l], d_skip[l])
        att = _att_decode(q, pool_kt, pool_vt, page_table, l * n_phys, sb_bias[l], dh, pps)
        xs = _merge(xs, y_s, att, *mrg, bsz_s)
        xs = _ffn(xs, *f2, bsz_s)
        outs[4].append(k.reshape(bsz_s, t_s, heads, dh))
        outs[5].append(v.reshape(bsz_s, t_s, heads, dh))
        outs[6].append(hr.reshape(bsz_s, g, p))
        outs[7].append(hi.reshape(bsz_s, g, p))

    st = [jnp.stack(o) for o in outs]
    return (xp.reshape(bsz_p, t, d), xs.reshape(bsz_s, t_s, d), st[0], st[1], st[2], st[3], st[4], st[5], st[6], st[7])
```

```python
import functools
import math

import jax
import jax.numpy as jnp
from jax import lax
from jax.experimental import pallas as pl
from jax.experimental.pallas import tpu as pltpu

F32 = jnp.float32
BF16 = jnp.bfloat16
RMS_EPS = 1e-6
LOG2_E = 1.4426950408889634
EXP2_CLAMP = 126.0
MASKED_LOG2 = -1e30

LANES = 128
VMEM_LIMIT = 56 * 1024 * 1024

S5_CHUNK = 16
S5_ROWS = 256
ATT_TQ = 256
ATT_TK = 256
ROW_TILE = 512
DEC_PAGES_PER_STEP = 8


def _cparams(*sem):
    return pltpu.CompilerParams(dimension_semantics=sem, vmem_limit_bytes=VMEM_LIMIT)


def _full(shape):
    n = len(shape)
    return pl.BlockSpec(shape, lambda *_: (0,) * n, pipeline_mode=pl.Buffered(1))


def _rms(x, g):
    return x * lax.rsqrt(jnp.mean(x * x, axis=-1, keepdims=True) + RMS_EPS) * g


def _dot(a, b):
    return jnp.dot(a, b, preferred_element_type=F32)


def _dot3(a, b):
    ah = a.astype(BF16)
    al = (a - ah.astype(F32)).astype(BF16)
    bh = b.astype(BF16)
    bl = (b - bh.astype(F32)).astype(BF16)
    return _dot(ah, bh) + _dot(ah, bl) + _dot(al, bh)


def _ffn_body(x_ref, pre_ref, post_ref, wup_ref, wdn_ref, o_ref, *, d_ff, fc):
    x = x_ref[...]
    u = _rms(x, pre_ref[...]).astype(BF16)
    acc = None
    for c in range(d_ff // fc):
        a = _dot(u, wup_ref[:, c * fc:(c + 1) * fc])
        b = _dot(u, wup_ref[:, d_ff + c * fc:d_ff + (c + 1) * fc])
        h = (a * jax.nn.sigmoid(a) * b).astype(BF16)
        p = _dot(h, wdn_ref[c * fc:(c + 1) * fc, :])
        acc = p if acc is None else acc + p
    o_ref[...] = x + 0.5 * _rms(acc, post_ref[...])


def _ffn(x, pre_g, post_g, w_up, w_dn, tm):
    n, d = x.shape
    d_ff = w_dn.shape[0]
    fc = d_ff // 2 if (d_ff // 2) % LANES == 0 else d_ff
    row = pl.BlockSpec((tm, d), lambda i: (i, 0))
    return pl.pallas_call(
        functools.partial(_ffn_body, d_ff=d_ff, fc=fc),
        grid=(n // tm,),
        in_specs=[row, _full((1, d)), _full((1, d)), _full((d, 2 * d_ff)), _full((d_ff, d))],
        out_specs=row,
        out_shape=jax.ShapeDtypeStruct((n, d), F32),
        compiler_params=_cparams("parallel"),
        name="ffn",
    )(x, pre_g, post_g, w_up, w_dn)


def _mixin_body(x_ref, g_ref, win_ref, s_ref, q_ref, k_ref, v_ref, vb_ref, *, ws, wa, q_scale):
    u = _rms(x_ref[...], g_ref[...]).astype(BF16)
    s_ref[...] = _dot(u, win_ref[:, :ws])
    q_ref[...] = (_dot(u, win_ref[:, ws:ws + wa]) * q_scale).astype(BF16)
    k_ref[...] = _dot(u, win_ref[:, ws + wa:ws + 2 * wa])
    v = _dot(u, win_ref[:, ws + 2 * wa:])
    v_ref[...] = v
    vb_ref[...] = v.astype(BF16)


def _mixin(x, g, w_in, ws, wa, q_scale, tm):
    n, d = x.shape
    row = lambda w: pl.BlockSpec((tm, w), lambda i: (i, 0))
    return pl.pallas_call(
        functools.partial(_mixin_body, ws=ws, wa=wa, q_scale=q_scale),
        grid=(n // tm,),
        in_specs=[row(d), _full((1, d)), _full(w_in.shape)],
        out_specs=[row(ws), row(wa), row(wa), row(wa), row(wa)],
        out_shape=[jax.ShapeDtypeStruct((n, ws), F32), jax.ShapeDtypeStruct((n, wa), BF16),
                   jax.ShapeDtypeStruct((n, wa), F32), jax.ShapeDtypeStruct((n, wa), F32),
                   jax.ShapeDtypeStruct((n, wa), BF16)],
        compiler_params=_cparams("parallel"),
        name="mixer_in",
    )(x, g, w_in)


def _s5_discretize(lam_re, lam_im, log_dt, b_re, b_im):
    dt = jnp.exp(log_dt)[:, None]
    mag = jnp.exp(lam_re * dt)
    lbr, lbi = mag * jnp.cos(lam_im * dt), mag * jnp.sin(lam_im * dt)
    nr, ni = lbr - 1.0, lbi
    den = lam_re * lam_re + lam_im * lam_im
    fr = ((nr * lam_re + ni * lam_im) / den)[:, :, None]
    fi = ((ni * lam_re - nr * lam_im) / den)[:, :, None]
    return lbr, lbi, fr * b_re - fi * b_im, fr * b_im + fi * b_re


def _s5_chunk_mats(lbr, lbi, bbr, bbi, c_re, c_im, n_l):
    g, p = lbr.shape
    c = bbr.shape[2]
    gb = LANES // c
    nb = g // gb
    eye = jnp.eye(gb, dtype=F32)
    hi = lax.Precision.HIGHEST
    pr, pi = [jnp.ones_like(lbr)], [jnp.zeros_like(lbr)]
    for _ in range(n_l):
        pr, pi = pr + [pr[-1] * lbr - pi[-1] * lbi], pi + [pr[-1] * lbi + pi[-1] * lbr]
    rev_r, rev_i = jnp.stack(pr[n_l - 1::-1]), jnp.stack(pi[n_l - 1::-1])
    pr, pi = jnp.stack(pr), jnp.stack(pi)
    clr = c_re[None] * pr[:, :, None, :] - c_im[None] * pi[:, :, None, :]
    cli = c_re[None] * pi[:, :, None, :] + c_im[None] * pr[:, :, None, :]
    gj = (jnp.einsum('jgcp,gpd->jgcd', clr[:n_l], bbr, precision=hi)
          - jnp.einsum('jgcp,gpd->jgcd', cli[:n_l], bbi, precision=hi))
    gr = gj[jnp.arange(n_l - 1, -1, -1)].reshape(n_l, nb, gb, c, c).transpose(1, 0, 2, 4, 3)
    r = gr[:, :, :, :, None, :] * eye[None, None, :, None, :, None]
    r = r.reshape(nb, n_l * LANES, LANES).astype(BF16)
    br = rev_r[:, :, :, None] * bbr[None] - rev_i[:, :, :, None] * bbi[None]
    bi = rev_r[:, :, :, None] * bbi[None] + rev_i[:, :, :, None] * bbr[None]

    def bd_b(b):
        b = b.reshape(n_l, nb, gb, p, c).transpose(1, 0, 2, 4, 3)
        b = b[:, :, :, :, None, :] * eye[None, None, :, None, :, None]
        return b.reshape(nb, n_l * LANES, gb * p)

    bm = jnp.concatenate([bd_b(br), bd_b(bi)], axis=-1).astype(BF16)

    def bd_c(m_):
        m_ = m_.reshape(n_l, nb, gb, c, p).transpose(1, 2, 4, 0, 3)
        m_ = m_[:, :, :, :, None, :] * eye[None, :, None, None, :, None]
        return m_.reshape(nb, gb * p, n_l * LANES).astype(BF16)

    return r, bm, bd_c(clr[1:]), bd_c(-cli[1:]), pr[n_l].reshape(1, g * p), pi[n_l].reshape(1, g * p)


def _gather_steps(u_ref, ucat_ref, n_l):
    for s in range(n_l):
        ucat_ref[:, s * LANES:(s + 1) * LANES] = u_ref[:, s, :].astype(BF16)


def _s5_local_body(u_ref, b_ref, sre_ref, sim_ref, ucat_ref, *, n_l):
    _gather_steps(u_ref, ucat_ref, n_l)
    st = _dot(ucat_ref[...], b_ref[...])
    half = st.shape[1] // 2
    sre_ref[...] = st[:, :half]
    sim_ref[...] = st[:, half:]


def _s5_scan_body(sre_ref, sim_ref, ar_ref, ai_ref, h0r_ref, h0i_ref, hr_ref, hi_ref, fr_ref, fi_ref, *, nc):
    ar, ai = ar_ref[...], ai_ref[...]

    def step(k, carry):
        hr, hi = carry
        hr_ref[pl.ds(k, 1), :] = hr
        hi_ref[pl.ds(k, 1), :] = hi
        sr = sre_ref[pl.ds(k, 1), :]
        si = sim_ref[pl.ds(k, 1), :]
        return ar * hr - ai * hi + sr, ar * hi + ai * hr + si

    hr, hi = lax.fori_loop(0, nc, step, (h0r_ref[...], h0i_ref[...]), unroll=8)
    fr_ref[...] = hr
    fi_ref[...] = hi


def _s5_out_body(u_ref, r_ref, hr_ref, hi_ref, cr_ref, ci_ref, d_ref, y_ref, ucat_ref, *, n_l):
    _gather_steps(u_ref, ucat_ref, n_l)
    y = _dot(hr_ref[...].astype(BF16), cr_ref[...]) + _dot(hi_ref[...].astype(BF16), ci_ref[...])
    d = d_ref[...]
    for t in range(n_l):
        conv = _dot(ucat_ref[:, :(t + 1) * LANES], r_ref[(n_l - 1 - t) * LANES:, :])
        y_ref[:, t, :] = conv + y[:, t * LANES:(t + 1) * LANES] + d * u_ref[:, t, :]


def _s5_prompt(s_in, h0r, h0i, mats, d_skip, n_l, g, p):
    t, ws = s_in.shape
    r, bm, cr, ci, alr, ali = mats
    nb, nc = ws // LANES, t // n_l
    nr = min(S5_ROWS, nc)
    wp = g * p // nb
    u = s_in.reshape(nc, n_l, ws)
    u_spec = pl.BlockSpec((nr, n_l, LANES), lambda j, r: (r, 0, j))
    st_spec = pl.BlockSpec((nr, wp), lambda j, r: (r, j))
    mat = lambda a: pl.BlockSpec((None,) + a.shape[1:], lambda j, r: (j, 0, 0))
    ucat = pltpu.VMEM((nr, n_l * LANES), BF16)
    state = jax.ShapeDtypeStruct((nc, g * p), F32)
    sre, sim = pl.pallas_call(
        functools.partial(_s5_local_body, n_l=n_l),
        grid=(nb, nc // nr),
        in_specs=[u_spec, mat(bm)],
        out_specs=[st_spec, st_spec],
        out_shape=[state, state],
        scratch_shapes=[ucat],
        compiler_params=_cparams("parallel", "parallel"),
        name="s5_local",
    )(u, bm)
    sw = 4 * LANES
    col = lambda r, w: pl.BlockSpec((r, w), lambda i: (0, i))
    last = jax.ShapeDtypeStruct((1, g * p), F32)
    hr, hi, fr, fi = pl.pallas_call(
        functools.partial(_s5_scan_body, nc=nc),
        grid=(g * p // sw,),
        in_specs=[col(nc, sw), col(nc, sw), col(1, sw), col(1, sw), col(1, sw), col(1, sw)],
        out_specs=[col(nc, sw), col(nc, sw), col(1, sw), col(1, sw)],
        out_shape=[state, state, last, last],
        compiler_params=_cparams("parallel"),
        name="s5_scan",
    )(sre, sim, alr, ali, h0r, h0i)
    y = pl.pallas_call(
        functools.partial(_s5_out_body, n_l=n_l),
        grid=(nb, nc // nr),
        in_specs=[u_spec, mat(r), st_spec, st_spec, mat(cr), mat(ci),
                  pl.BlockSpec((1, LANES), lambda j, r: (0, j))],
        out_specs=u_spec,
        out_shape=jax.ShapeDtypeStruct((nc, n_l, ws), F32),
        scratch_shapes=[ucat],
        compiler_params=_cparams("parallel", "parallel"),
        name="s5_out",
    )(u, r, hr, hi, cr, ci, d_skip.reshape(1, ws))
    return y.reshape(t, ws), fr, fi


def _s5_step_body(u_ref, h0r_ref, h0i_ref, lr_ref, li_ref, br_ref, bi_ref, cr_ref, ci_ref, d_ref,
                  y_ref, hr_ref, hi_ref):
    u = u_ref[...]
    h0r, h0i, lr, li = h0r_ref[...], h0i_ref[...], lr_ref[...], li_ref[...]
    hr = lr * h0r - li * h0i + _dot3(u, br_ref[...])
    hi = lr * h0i + li * h0r + _dot3(u, bi_ref[...])
    hr_ref[...] = hr
    hi_ref[...] = hi
    y_ref[...] = _dot3(hr, cr_ref[...]) - _dot3(hi, ci_ref[...]) + d_ref[...] * u


def _s5_step(u, h0r, h0i, lbr, lbi, bbr, bbi, c_re, c_im, d_skip):
    bsz = u.shape[0]
    g, p, c = bbr.shape
    gb = LANES // c
    nb = g // gb
    eye = jnp.eye(gb, dtype=F32)

    def bd_in(b):
        return jnp.einsum('kapc,ab->kacbp', b.reshape(nb, gb, p, c), eye).reshape(nb, gb * c, gb * p)

    def bd_out(m):
        return jnp.einsum('kacp,ab->kapbc', m.reshape(nb, gb, c, p), eye).reshape(nb, gb * p, gb * c)

    blk = lambda r, w: pl.BlockSpec((r, w), lambda k: (0, k))
    grp = lambda r, w: pl.BlockSpec((None, r, w), lambda k: (k, 0, 0))
    wc, wp = gb * c, gb * p
    return pl.pallas_call(
        _s5_step_body,
        grid=(nb,),
        in_specs=[blk(bsz, wc), blk(bsz, wp), blk(bsz, wp), blk(1, wp), blk(1, wp),
                  grp(wc, wp), grp(wc, wp), grp(wp, wc), grp(wp, wc), blk(1, wc)],
        out_specs=[blk(bsz, wc), blk(bsz, wp), blk(bsz, wp)],
        out_shape=[jax.ShapeDtypeStruct((bsz, g * c), F32), jax.ShapeDtypeStruct((bsz, g * p), F32),
                   jax.ShapeDtypeStruct((bsz, g * p), F32)],
        compiler_params=_cparams("parallel"),
        name="s5_step",
    )(u, h0r, h0i, lbr.reshape(1, g * p), lbi.reshape(1, g * p), bd_in(bbr), bd_in(bbi),
      bd_out(c_re), bd_out(c_im), d_skip.reshape(1, g * c))


def _softplus2(z):
    return jnp.maximum(jnp.log(1.0 + jnp.exp2(jnp.minimum(z, EXP2_CLAMP))) * LOG2_E, z)


def _att_prompt_body(bias_ref, q_ref, kt_ref, v_ref, tri_ref, o_ref,
                     lb0, lb1, lb2, lb3, sp0, sp1, sp2, sp3, rs0, rs1, rs2, rs3, w0, w1, w2, w3, acc_ref, run_ref,
                     *, tq, tk, dh):
    lbs, sps, rss, wbs = (lb0, lb1, lb2, lb3), (sp0, sp1, sp2, sp3), (rs0, rs1, rs2, rs3), (w0, w1, w2, w3)
    pair = pl.program_id(0)
    qi = pl.program_id(1)
    q = q_ref[...]
    heads = q.shape[1] // dh
    lane = lax.broadcasted_iota(jnp.int32, q.shape, 1)
    in_head = [(lane >= hh * dh) & (lane < (hh + 1) * dh) for hh in range(heads)]
    q2 = jnp.concatenate([jnp.where(m, q, jnp.zeros_like(q)) for m in in_head], axis=0)
    bias = [bias_ref[pair * heads + hh] * LOG2_E for hh in range(heads)]

    nd = tq // tk
    last = (qi + 1) * nd - 1

    def keys_of(b):
        j = jnp.clip(last - b, 0, last)
        return pl.ds(pl.multiple_of(j * tk, tk), tk)

    def scores(slot, s, penalty, key_offset):
        for hh in range(heads):
            r = slice(hh * tq, (hh + 1) * tq)
            z = s[r, :] + (bias[hh] + penalty)
            sp = _softplus2(z)
            lb = z - sp
            if key_offset is not None:
                col = lax.broadcasted_iota(jnp.int32, z.shape, 1) + key_offset
                valid = col < lax.broadcasted_iota(jnp.int32, z.shape, 0)
                sp = jnp.where(valid, sp, 0.0)
                lb = jnp.where(valid, lb, MASKED_LOG2)
            lbs[slot][r, :] = lb
            sps[slot][r, :] = sp.astype(BF16)
            rss[slot][r, :] = jnp.sum(sp, axis=1, keepdims=True)

    def trip(t, slot):
        prev = (slot - 2) % 4
        later = _dot(sps[prev][...], tri_ref[...])
        s = _dot(q2, kt_ref[:, keys_of(t)])
        acc_ref[...] += _dot(wbs[slot][...], v_ref[keys_of(t - 4), :])
        wbs[prev][...] = jnp.exp2(lbs[prev][...] - later - run_ref[...]).astype(BF16)
        run_ref[...] += rss[prev][...]
        scores(slot, s, jnp.where(t > last, MASKED_LOG2, 0.0), None)

    acc_ref[...] = jnp.zeros_like(acc_ref)
    run_ref[...] = jnp.zeros_like(run_ref)
    for w in wbs:
        w[...] = jnp.zeros_like(w)
    for b in range(nd):
        scores(b, _dot(q2, kt_ref[:, keys_of(b)]), 0.0, (nd - 1 - b) * tk)
    for b in range(nd - 2, 0):
        lbs[b % 4][...] = jnp.full(lbs[b % 4].shape, MASKED_LOG2, F32)
        sps[b % 4][...] = jnp.zeros_like(sps[b % 4])
        rss[b % 4][...] = jnp.zeros_like(rss[b % 4])

    def trips(i, _, n):
        for k in range(n):
            trip(nd + 4 * i + k, (nd + k) % 4)
        return 0

    pairs = (last + 6 - nd) // 2
    lax.fori_loop(0, pairs // 2, functools.partial(trips, n=4), 0)
    lax.fori_loop(pairs // 2, (pairs + 1) // 2, functools.partial(trips, n=2), 0)
    acc = acc_ref[...]
    out = acc[:tq]
    for hh in range(1, heads):
        out = jnp.where(in_head[hh], acc[hh * tq:(hh + 1) * tq], out)
    o_ref[...] = out.astype(o_ref.dtype)


def _att_prompt(q, kt, v, bias, dh, tq, tk):
    assert tq in (tk, 2 * tk)
    t, w = q.shape
    rows = (LANES // dh) * tq
    tri = (jnp.arange(tk)[:, None] > jnp.arange(tk)[None, :]).astype(BF16)
    ring = lambda shape, dtype: [pltpu.VMEM(shape, dtype)] * 4
    return pl.pallas_call(
        functools.partial(_att_prompt_body, tq=tq, tk=tk, dh=dh),
        grid=(w // LANES, t // tq),
        in_specs=[pl.BlockSpec(memory_space=pltpu.SMEM),
                  pl.BlockSpec((tq, LANES), lambda p, i: (i, p)),
                  pl.BlockSpec((LANES, t), lambda p, i: (p, 0)),
                  pl.BlockSpec((t, LANES), lambda p, i: (0, p)),
                  _full((tk, tk))],
        out_specs=pl.BlockSpec((tq, LANES), lambda p, i: (i, p)),
        out_shape=jax.ShapeDtypeStruct((t, w), BF16),
        scratch_shapes=(ring((rows, tk), F32) + ring((rows, tk), BF16) + ring((rows, 1), F32)
                        + ring((rows, tk), BF16) + [pltpu.VMEM((rows, LANES), F32), pltpu.VMEM((rows, 1), F32)]),
        compiler_params=_cparams("parallel", "parallel"),
        name="att_prompt",
    )(bias, q, kt, v, tri)


def _att_decode_body(pt_ref, q_ref, bias_ref, tri_ref, *rest, pps, dh):
    kt_refs, vt_refs = rest[:pps], rest[pps:2 * pps]
    o_ref, acc_ref, run_ref = rest[2 * pps:]
    step = pl.program_id(1)
    w, page = acc_ref.shape
    heads = w // dh

    @pl.when(step == 0)
    def _():
        acc_ref[...] = jnp.zeros_like(acc_ref)
        run_ref[...] = jnp.zeros_like(run_ref)

    head_of_lane = lax.broadcasted_iota(jnp.int32, (heads, w), 1) // dh
    own = head_of_lane == lax.broadcasted_iota(jnp.int32, (heads, w), 0)
    qbd = jnp.where(own, jnp.broadcast_to(q_ref[0].astype(F32), (heads, w)), 0.0).astype(BF16)
    bias = bias_ref[...]
    tri = tri_ref[...]
    run = run_ref[...]
    wgt = []
    for r in range(pps):
        z = _dot(qbd, kt_refs[r][0].astype(BF16)) + bias
        sp = _softplus2(z)
        sp_hi = sp.astype(BF16)
        sp_lo = (sp - sp_hi.astype(F32)).astype(BF16)
        later = _dot(sp_hi, tri) + _dot(sp_lo, tri)
        wgt.append(jnp.exp2(z - sp - later - run))
        run = run + jnp.sum(sp, axis=1, keepdims=True)
    run_ref[...] = run
    for hh in range(heads):
        rows = slice(hh * dh, (hh + 1) * dh)
        a = acc_ref[rows, :]
        for r in range(pps):
            a = a + vt_refs[r][0, rows, :] * jnp.broadcast_to(wgt[r][hh:hh + 1, :], (dh, page))
        acc_ref[rows, :] = a

    @pl.when(step == pl.num_programs(1) - 1)
    def _():
        o_ref[0] = jnp.sum(acc_ref[...], axis=1, keepdims=True)


def _att_decode(q, pool_kt, pool_vt, page_table, page_base, bias, dh, pps):
    bsz, w = q.shape
    n_pages = page_table.shape[1]
    page = pool_kt.shape[2]
    heads = w // dh
    assert n_pages % pps == 0
    pt = (page_table + page_base).reshape(-1)
    tri = (jnp.arange(page)[:, None] > jnp.arange(page)[None, :]).astype(BF16)
    bias_b = jnp.broadcast_to((bias * LOG2_E)[:, None], (heads, page))

    def page_spec(r):
        return pl.BlockSpec((1, w, page), lambda b, j, pt_ref: (pt_ref[b * n_pages + n_pages - 1 - (j * pps + r)], 0, 0))

    cache_specs = [page_spec(r) for r in range(pps)]
    out = pl.pallas_call(
        functools.partial(_att_decode_body, pps=pps, dh=dh),
        grid_spec=pltpu.PrefetchScalarGridSpec(
            num_scalar_prefetch=1,
            grid=(bsz, n_pages // pps),
            in_specs=[pl.BlockSpec((1, 1, w), lambda b, j, pt_ref: (b, 0, 0)),
                      pl.BlockSpec((heads, page), lambda b, j, pt_ref: (0, 0)),
                      pl.BlockSpec((page, page), lambda b, j, pt_ref: (0, 0))] + cache_specs + cache_specs,
            out_specs=pl.BlockSpec((1, w, 1), lambda b, j, pt_ref: (b, 0, 0)),
            scratch_shapes=[pltpu.VMEM((w, page), F32), pltpu.VMEM((heads, 1), F32)],
        ),
        out_shape=jax.ShapeDtypeStruct((bsz, w, 1), F32),
        compiler_params=_cparams("parallel", "arbitrary"),
        name="att_decode",
    )(pt, q.reshape(bsz, 1, w), bias_b, tri, *([pool_kt] * pps), *([pool_vt] * pps))
    return out.reshape(bsz, w)


def _merge_body(x_ref, ys_ref, att_ref, pre_ref, post_ref, wglu_ref, wbs_ref, wba_ref, wg_ref, bg_ref, wo_ref,
                o_ref):
    x = x_ref[...]
    d = x.shape[1]
    ws = ys_ref.shape[1]
    u = _rms(x, pre_ref[...]).astype(BF16)
    glu = _dot(jax.nn.gelu(ys_ref[...]).astype(BF16), wglu_ref[...])
    y_s = (glu[:, :ws] * jax.nn.sigmoid(glu[:, ws:])).astype(BF16)
    o_s = _dot(y_s, wbs_ref[...])
    o_a = _dot(att_ref[...].astype(BF16), wba_ref[...])
    g_s = jax.nn.sigmoid(_dot(u, wg_ref[:, :d]) + bg_ref[:, :d])
    g_a = jax.nn.sigmoid(_dot(u, wg_ref[:, d:]) + bg_ref[:, d:])
    merged = (g_s * o_s + g_a * o_a).astype(BF16)
    o_ref[...] = x + _rms(_dot(merged, wo_ref[...]), post_ref[...])


def _merge(x, ys, att, pre_g, post_g, w_glu, w_bs, w_ba, w_gate, b_gate, w_out, tm):
    n, d = x.shape
    row = lambda a: pl.BlockSpec((tm, a.shape[1]), lambda i: (i, 0))
    weights = (pre_g, post_g, w_glu, w_bs, w_ba, w_gate, b_gate, w_out)
    return pl.pallas_call(
        _merge_body,
        grid=(n // tm,),
        in_specs=[row(x), row(ys), row(att)] + [_full(a.shape) for a in weights],
        out_specs=row(x),
        out_shape=jax.ShapeDtypeStruct((n, d), F32),
        compiler_params=_cparams("parallel"),
        name="merge",
    )(x, ys, att, *weights)


def kernel(x_prompt, x_sample, cache_k, cache_v, state_ssm_re, state_ssm_im, page_table, ffn1_pre_g, ffn1_post_g, ffn1_w_up, ffn1_w_down, mix_pre_g, mix_post_g, w_in, sb_bias, lam_re, lam_im, log_dt, b_re, b_im, c_re, c_im, d_skip, w_glu, w_br_ssm, w_br_att, w_gate, b_gate, w_out, ffn2_pre_g, ffn2_post_g, ffn2_w_up, ffn2_w_down):
    bsz_p, t, d = x_prompt.shape
    bsz_s, t_s, _ = x_sample.shape
    assert bsz_p == 1 and t_s == 1
    depth, n_phys, page, heads, dh = cache_k.shape
    _, g, p = lam_re.shape
    c = b_re.shape[3]
    ws, wa = g * c, heads * dh
    q_scale = LOG2_E / math.sqrt(dh)
    tm = min(ROW_TILE, t)
    pps = math.gcd(DEC_PAGES_PER_STEP, page_table.shape[1])

    xp = x_prompt.reshape(t, d)
    xs = x_sample.reshape(bsz_s, d)
    pool_kt = cache_k.transpose(0, 1, 3, 4, 2).reshape(depth * n_phys, wa, page)
    pool_vt = cache_v.transpose(0, 1, 3, 4, 2).reshape(depth * n_phys, wa, page)
    row = lambda a: a.reshape(1, -1)
    zeros_state = jnp.zeros((1, g * p), F32)

    outs = [[] for _ in range(8)]
    for l in range(depth):
        wb = lambda a: a[l].astype(BF16)
        f1 = (row(ffn1_pre_g[l]), row(ffn1_post_g[l]), wb(ffn1_w_up), wb(ffn1_w_down))
        f2 = (row(ffn2_pre_g[l]), row(ffn2_post_g[l]), wb(ffn2_w_up), wb(ffn2_w_down))
        w_in_l = wb(w_in)
        mrg = (row(mix_pre_g[l]), row(mix_post_g[l]), wb(w_glu), wb(w_br_ssm), wb(w_br_att), wb(w_gate),
               row(b_gate[l]), wb(w_out))
        lbr, lbi, bbr, bbi = _s5_discretize(lam_re[l], lam_im[l], log_dt[l], b_re[l], b_im[l])
        mats = _s5_chunk_mats(lbr, lbi, bbr, bbi, c_re[l], c_im[l], S5_CHUNK)

        xp = _ffn(xp, *f1, tm)
        s_in, q, k, v, vb = _mixin(xp, row(mix_pre_g[l]), w_in_l, ws, wa, q_scale, tm)
        y_s, hr, hi = _s5_prompt(s_in, zeros_state, zeros_state, mats, d_skip[l], S5_CHUNK, g, p)
        att = _att_prompt(q, k.T.astype(BF16), vb, sb_bias[l], dh, min(ATT_TQ, t), min(ATT_TK, t))
        xp = _merge(xp, y_s, att, *mrg, tm)
        xp = _ffn(xp, *f2, tm)
        outs[0].append(k.reshape(bsz_p, t, heads, dh))
        outs[1].append(v.reshape(bsz_p, t, heads, dh))
        outs[2].append(hr.reshape(bsz_p, g, p))
        outs[3].append(hi.reshape(bsz_p, g, p))

        xs = _ffn(xs, *f1, bsz_s)
        s_in, q, k, v, _ = _mixin(xs, row(mix_pre_g[l]), w_in_l, ws, wa, q_scale, bsz_s)
        y_s, hr, hi = _s5_step(s_in, state_ssm_re[l].reshape(bsz_s, g * p), state_ssm_im[l].reshape(bsz_s, g * p),
                               lbr, lbi, bbr, bbi, c_re[l], c_im[l], d_skip[l])
        att = _att_decode(q, pool_kt, pool_vt, page_table, l * n_phys, sb_bias[l], dh, pps)
        xs = _merge(xs, y_s, att, *mrg, bsz_s)
        xs = _ffn(xs, *f2, bsz_s)
        outs[4].append(k.reshape(bsz_s, t_s, heads, dh))
        outs[5].append(v.reshape(bsz_s, t_s, heads, dh))
        outs[6].append(hr.reshape(bsz_s, g, p))
        outs[7].append(hi.reshape(bsz_s, g, p))

    st = [jnp.stack(o) for o in outs]
    return (xp.reshape(bsz_p, t, d), xs.reshape(bsz_s, t_s, d), st[0], st[1], st[2], st[3], st[4], st[5], st[6], st[7])
```

```python
import functools
import math

import jax
import jax.numpy as jnp
from jax import lax
from jax.experimental import pallas as pl
from jax.experimental.pallas import tpu as pltpu

F32 = jnp.float32
BF16 = jnp.bfloat16
RMS_EPS = 1e-6
LOG2_E = 1.4426950408889634
EXP2_CLAMP = 126.0
MASKED_LOG2 = -1e30

LANES = 128
VMEM_LIMIT = 56 * 1024 * 1024

S5_CHUNK = 16
S5_ROWS = 256
ATT_TQ = 256
ATT_TK = 256
ROW_TILE = 512
DEC_PAGES_PER_STEP = 8


def _cparams(*sem):
    return pltpu.CompilerParams(dimension_semantics=sem, vmem_limit_bytes=VMEM_LIMIT)


def _full(shape):
    n = len(shape)
    return pl.BlockSpec(shape, lambda *_: (0,) * n, pipeline_mode=pl.Buffered(1))


def _rms(x, g):
    return x * lax.rsqrt(jnp.mean(x * x, axis=-1, keepdims=True) + RMS_EPS) * g


def _dot(a, b):
    return jnp.dot(a, b, preferred_element_type=F32)


def _dot3(a, b):
    ah = a.astype(BF16)
    al = (a - ah.astype(F32)).astype(BF16)
    bh = b.astype(BF16)
    bl = (b - bh.astype(F32)).astype(BF16)
    return _dot(ah, bh) + _dot(ah, bl) + _dot(al, bh)


def _ffn_body(x_ref, pre_ref, post_ref, wup_ref, wdn_ref, o_ref, *, d_ff, fc):
    x = x_ref[...]
    u = _rms(x, pre_ref[...]).astype(BF16)
    acc = None
    for c in range(d_ff // fc):
        a = _dot(u, wup_ref[:, c * fc:(c + 1) * fc])
        b = _dot(u, wup_ref[:, d_ff + c * fc:d_ff + (c + 1) * fc])
        h = (a * jax.nn.sigmoid(a) * b).astype(BF16)
        p = _dot(h, wdn_ref[c * fc:(c + 1) * fc, :])
        acc = p if acc is None else acc + p
    o_ref[...] = x + 0.5 * _rms(acc, post_ref[...])


def _ffn(x, pre_g, post_g, w_up, w_dn, tm):
    n, d = x.shape
    d_ff = w_dn.shape[0]
    fc = d_ff // 2 if (d_ff // 2) % LANES == 0 else d_ff
    row = pl.BlockSpec((tm, d), lambda i: (i, 0))
    return pl.pallas_call(
        functools.partial(_ffn_body, d_ff=d_ff, fc=fc),
        grid=(n // tm,),
        in_specs=[row, _full((1, d)), _full((1, d)), _full((d, 2 * d_ff)), _full((d_ff, d))],
        out_specs=row,
        out_shape=jax.ShapeDtypeStruct((n, d), F32),
        compiler_params=_cparams("parallel"),
        name="ffn",
    )(x, pre_g, post_g, w_up, w_dn)


def _mixin_body(x_ref, g_ref, win_ref, s_ref, q_ref, k_ref, v_ref, vb_ref, *, ws, wa, q_scale):
    u = _rms(x_ref[...], g_ref[...]).astype(BF16)
    s_ref[...] = _dot(u, win_ref[:, :ws])
    q_ref[...] = (_dot(u, win_ref[:, ws:ws + wa]) * q_scale).astype(BF16)
    k_ref[...] = _dot(u, win_ref[:, ws + wa:ws + 2 * wa])
    v = _dot(u, win_ref[:, ws + 2 * wa:])
    v_ref[...] = v
    vb_ref[...] = v.astype(BF16)


def _mixin(x, g, w_in, ws, wa, q_scale, tm):
    n, d = x.shape
    row = lambda w: pl.BlockSpec((tm, w), lambda i: (i, 0))
    return pl.pallas_call(
        functools.partial(_mixin_body, ws=ws, wa=wa, q_scale=q_scale),
        grid=(n // tm,),
        in_specs=[row(d), _full((1, d)), _full(w_in.shape)],
        out_specs=[row(ws), row(wa), row(wa), row(wa), row(wa)],
        out_shape=[jax.ShapeDtypeStruct((n, ws), F32), jax.ShapeDtypeStruct((n, wa), BF16),
                   jax.ShapeDtypeStruct((n, wa), F32), jax.ShapeDtypeStruct((n, wa), F32),
                   jax.ShapeDtypeStruct((n, wa), BF16)],
        compiler_params=_cparams("parallel"),
        name="mixer_in",
    )(x, g, w_in)


def _s5_discretize(lam_re, lam_im, log_dt, b_re, b_im):
    dt = jnp.exp(log_dt)[:, None]
    mag = jnp.exp(lam_re * dt)
    lbr, lbi = mag * jnp.cos(lam_im * dt), mag * jnp.sin(lam_im * dt)
    nr, ni = lbr - 1.0, lbi
    den = lam_re * lam_re + lam_im * lam_im
    fr = ((nr * lam_re + ni * lam_im) / den)[:, :, None]
    fi = ((ni * lam_re - nr * lam_im) / den)[:, :, None]
    return lbr, lbi, fr * b_re - fi * b_im, fr * b_im + fi * b_re


def _s5_chunk_mats(lbr, lbi, bbr, bbi, c_re, c_im, n_l):
    g, p = lbr.shape
    c = bbr.shape[2]
    gb = LANES // c
    nb = g // gb
    eye = jnp.eye(gb, dtype=F32)
    hi = lax.Precision.HIGHEST
    pr, pi = [jnp.ones_like(lbr)], [jnp.zeros_like(lbr)]
    for _ in range(n_l):
        pr, pi = pr + [pr[-1] * lbr - pi[-1] * lbi], pi + [pr[-1] * lbi + pi[-1] * lbr]
    rev_r, rev_i = jnp.stack(pr[n_l - 1::-1]), jnp.stack(pi[n_l - 1::-1])
    pr, pi = jnp.stack(pr), jnp.stack(pi)
    clr = c_re[None] * pr[:, :, None, :] - c_im[None] * pi[:, :, None, :]
    cli = c_re[None] * pi[:, :, None, :] + c_im[None] * pr[:, :, None, :]
    gj = (jnp.einsum('jgcp,gpd->jgcd', clr[:n_l], bbr, precision=hi)
          - jnp.einsum('jgcp,gpd->jgcd', cli[:n_l], bbi, precision=hi))
    gr = gj[jnp.arange(n_l - 1, -1, -1)].reshape(n_l, nb, gb, c, c).transpose(1, 0, 2, 4, 3)
    r = gr[:, :, :, :, None, :] * eye[None, None, :, None, :, None]
    r = r.reshape(nb, n_l * LANES, LANES).astype(BF16)
    br = rev_r[:, :, :, None] * bbr[None] - rev_i[:, :, :, None] * bbi[None]
    bi = rev_r[:, :, :, None] * bbi[None] + rev_i[:, :, :, None] * bbr[None]

    def bd_b(b):
        b = b.reshape(n_l, nb, gb, p, c).transpose(1, 0, 2, 4, 3)
        b = b[:, :, :, :, None, :] * eye[None, None, :, None, :, None]
        return b.reshape(nb, n_l * LANES, gb * p)

    bm = jnp.concatenate([bd_b(br), bd_b(bi)], axis=-1).astype(BF16)

    def bd_c(m_):
        m_ = m_.reshape(n_l, nb, gb, c, p).transpose(1, 2, 4, 0, 3)
        m_ = m_[:, :, :, :, None, :] * eye[None, :, None, None, :, None]
        return m_.reshape(nb, gb * p, n_l * LANES).astype(BF16)

    return r, bm, bd_c(clr[1:]), bd_c(-cli[1:]), pr[n_l].reshape(1, g * p), pi[n_l].reshape(1, g * p)


def _gather_steps(u_ref, ucat_ref, n_l):
    for s in range(n_l):
        ucat_ref[:, s * LANES:(s + 1) * LANES] = u_ref[:, s, :].astype(BF16)


def _s5_local_body(u_ref, b_ref, sre_ref, sim_ref, ucat_ref, *, n_l):
    _gather_steps(u_ref, ucat_ref, n_l)
    st = _dot(ucat_ref[...], b_ref[...])
    half = st.shape[1] // 2
    sre_ref[...] = st[:, :half]
    sim_ref[...] = st[:, half:]


def _s5_scan_body(sre_ref, sim_ref, ar_ref, ai_ref, h0r_ref, h0i_ref, hr_ref, hi_ref, fr_ref, fi_ref, *, nc):
    ar, ai = ar_ref[...], ai_ref[...]

    def step(k, carry):
        hr, hi = carry
        hr_ref[pl.ds(k, 1), :] = hr
        hi_ref[pl.ds(k, 1), :] = hi
        sr = sre_ref[pl.ds(k, 1), :]
        si = sim_ref[pl.ds(k, 1), :]
        return ar * hr - ai * hi + sr, ar * hi + ai * hr + si

    hr, hi = lax.fori_loop(0, nc, step, (h0r_ref[...], h0i_ref[...]), unroll=8)
    fr_ref[...] = hr
    fi_ref[...] = hi


def _s5_out_body(u_ref, r_ref, hr_ref, hi_ref, cr_ref, ci_ref, d_ref, y_ref, ucat_ref, *, n_l):
    _gather_steps(u_ref, ucat_ref, n_l)
    y = _dot(hr_ref[...].astype(BF16), cr_ref[...]) + _dot(hi_ref[...].astype(BF16), ci_ref[...])
    d = d_ref[...]
    for t in range(n_l):
        conv = _dot(ucat_ref[:, :(t + 1) * LANES], r_ref[(n_l - 1 - t) * LANES:, :])
        y_ref[:, t, :] = conv + y[:, t * LANES:(t + 1) * LANES] + d * u_ref[:, t, :]


def _s5_prompt(s_in, h0r, h0i, mats, layer, d_skip, n_l, g, p):
    t, ws = s_in.shape
    r, bm, cr, ci, alr, ali = mats
    nb, nc = ws // LANES, t // n_l
    nr = min(S5_ROWS, nc)
    wp = g * p // nb
    u = s_in.reshape(nc, n_l, ws)
    u_spec = pl.BlockSpec((nr, n_l, LANES), lambda j, r: (r, 0, j))
    st_spec = pl.BlockSpec((nr, wp), lambda j, r: (r, j))
    mat = lambda a: pl.BlockSpec((None, None) + a.shape[2:], lambda j, r: (layer, j, 0, 0))
    ucat = pltpu.VMEM((nr, n_l * LANES), BF16)
    state = jax.ShapeDtypeStruct((nc, g * p), F32)
    sre, sim = pl.pallas_call(
        functools.partial(_s5_local_body, n_l=n_l),
        grid=(nb, nc // nr),
        in_specs=[u_spec, mat(bm)],
        out_specs=[st_spec, st_spec],
        out_shape=[state, state],
        scratch_shapes=[ucat],
        compiler_params=_cparams("parallel", "parallel"),
        name="s5_local",
    )(u, bm)
    sw = 4 * LANES
    col = lambda r, w: pl.BlockSpec((r, w), lambda i: (0, i))
    last = jax.ShapeDtypeStruct((1, g * p), F32)
    hr, hi, fr, fi = pl.pallas_call(
        functools.partial(_s5_scan_body, nc=nc),
        grid=(g * p // sw,),
        in_specs=[col(nc, sw), col(nc, sw), col(1, sw), col(1, sw), col(1, sw), col(1, sw)],
        out_specs=[col(nc, sw), col(nc, sw), col(1, sw), col(1, sw)],
        out_shape=[state, state, last, last],
        compiler_params=_cparams("parallel"),
        name="s5_scan",
    )(sre, sim, alr[layer], ali[layer], h0r, h0i)
    y = pl.pallas_call(
        functools.partial(_s5_out_body, n_l=n_l),
        grid=(nb, nc // nr),
        in_specs=[u_spec, mat(r), st_spec, st_spec, mat(cr), mat(ci),
                  pl.BlockSpec((1, LANES), lambda j, r: (0, j))],
        out_specs=u_spec,
        out_shape=jax.ShapeDtypeStruct((nc, n_l, ws), F32),
        scratch_shapes=[ucat],
        compiler_params=_cparams("parallel", "parallel"),
        name="s5_out",
    )(u, r, hr, hi, cr, ci, d_skip.reshape(1, ws))
    return y.reshape(t, ws), fr, fi


def _s5_step_body(u_ref, h0r_ref, h0i_ref, lr_ref, li_ref, br_ref, bi_ref, cr_ref, ci_ref, d_ref,
                  y_ref, hr_ref, hi_ref):
    u = u_ref[...]
    h0r, h0i, lr, li = h0r_ref[...], h0i_ref[...], lr_ref[...], li_ref[...]
    hr = lr * h0r - li * h0i + _dot3(u, br_ref[...])
    hi = lr * h0i + li * h0r + _dot3(u, bi_ref[...])
    hr_ref[...] = hr
    hi_ref[...] = hi
    y_ref[...] = _dot3(hr, cr_ref[...]) - _dot3(hi, ci_ref[...]) + d_ref[...] * u


def _s5_step_mats(bbr, bbi, c_re, c_im):
    g, p, c = bbr.shape
    gb = LANES // c
    nb = g // gb
    eye = jnp.eye(gb, dtype=F32)

    def bd_in(b):
        return jnp.einsum('kapc,ab->kacbp', b.reshape(nb, gb, p, c), eye).reshape(nb, gb * c, gb * p)

    def bd_out(m):
        return jnp.einsum('kacp,ab->kapbc', m.reshape(nb, gb, c, p), eye).reshape(nb, gb * p, gb * c)

    return bd_in(bbr), bd_in(bbi), bd_out(c_re), bd_out(c_im)


def _s5_step(u, h0r, h0i, lbr, lbi, step_mats, layer, d_skip):
    bsz = u.shape[0]
    _, nb, wc, wp = step_mats[0].shape
    g_p, g_c = nb * wp, nb * wc
    blk = lambda r, w: pl.BlockSpec((r, w), lambda k: (0, k))
    grp = lambda r, w: pl.BlockSpec((None, None, r, w), lambda k: (layer, k, 0, 0))
    return pl.pallas_call(
        _s5_step_body,
        grid=(nb,),
        in_specs=[blk(bsz, wc), blk(bsz, wp), blk(bsz, wp), blk(1, wp), blk(1, wp),
                  grp(wc, wp), grp(wc, wp), grp(wp, wc), grp(wp, wc), blk(1, wc)],
        out_specs=[blk(bsz, wc), blk(bsz, wp), blk(bsz, wp)],
        out_shape=[jax.ShapeDtypeStruct((bsz, g_c), F32), jax.ShapeDtypeStruct((bsz, g_p), F32),
                   jax.ShapeDtypeStruct((bsz, g_p), F32)],
        compiler_params=_cparams("parallel"),
        name="s5_step",
    )(u, h0r, h0i, lbr.reshape(1, g_p), lbi.reshape(1, g_p), *step_mats, d_skip.reshape(1, g_c))


def _log2_gates(z):
    sp = jnp.maximum(jnp.log(1.0 + jnp.exp2(jnp.minimum(z, EXP2_CLAMP))) * LOG2_E, z)
    return z - sp, sp


def _att_prompt_body(bias_ref, q_ref, kt_ref, v_ref, tri_ref, o_ref,
                     lb0, lb1, lb2, lb3, sp0, sp1, sp2, sp3, rs0, rs1, rs2, rs3, w0, w1, w2, w3, acc_ref, run_ref,
                     *, tq, tk, dh):
    lbs, sps, rss, wbs = (lb0, lb1, lb2, lb3), (sp0, sp1, sp2, sp3), (rs0, rs1, rs2, rs3), (w0, w1, w2, w3)
    pair = pl.program_id(0)
    qi = pl.program_id(1)
    q = q_ref[...]
    heads = q.shape[1] // dh
    lane = lax.broadcasted_iota(jnp.int32, q.shape, 1)
    in_head = [(lane >= hh * dh) & (lane < (hh + 1) * dh) for hh in range(heads)]
    q2 = jnp.concatenate([jnp.where(m, q, jnp.zeros_like(q)) for m in in_head], axis=0)
    bias = [bias_ref[pair * heads + hh] * LOG2_E for hh in range(heads)]

    nd = tq // tk
    last = (qi + 1) * nd - 1

    def keys_of(b):
        j = jnp.clip(last - b, 0, last)
        return pl.ds(pl.multiple_of(j * tk, tk), tk)

    def scores(slot, s, penalty, key_offset):
        for hh in range(heads):
            r = slice(hh * tq, (hh + 1) * tq)
            lb, sp = _log2_gates(s[r, :] + (bias[hh] + penalty))
            if key_offset is not None:
                col = lax.broadcasted_iota(jnp.int32, sp.shape, 1) + key_offset
                valid = col < lax.broadcasted_iota(jnp.int32, sp.shape, 0)
                sp = jnp.where(valid, sp, 0.0)
                lb = jnp.where(valid, lb, MASKED_LOG2)
            lbs[slot][r, :] = lb
            sps[slot][r, :] = sp.astype(BF16)
            rss[slot][r, :] = sp[:, 0:1]

    def trip(t, slot):
        prev = (slot - 2) % 4
        later = _dot(sps[prev][...], tri_ref[...])
        s = _dot(q2, kt_ref[:, keys_of(t)])
        acc_ref[...] += _dot(wbs[slot][...], v_ref[keys_of(t - 4), :])
        wbs[prev][...] = jnp.exp2(lbs[prev][...] - later - run_ref[...]).astype(BF16)
        run_ref[...] += rss[prev][...] + later[:, 0:1]
        scores(slot, s, jnp.where(t > last, MASKED_LOG2, 0.0), None)

    acc_ref[...] = jnp.zeros_like(acc_ref)
    run_ref[...] = jnp.zeros_like(run_ref)
    for w in wbs:
        w[...] = jnp.zeros_like(w)
    for b in range(nd):
        scores(b, _dot(q2, kt_ref[:, keys_of(b)]), 0.0, (nd - 1 - b) * tk)
    for b in range(nd - 2, 0):
        lbs[b % 4][...] = jnp.full(lbs[b % 4].shape, MASKED_LOG2, F32)
        sps[b % 4][...] = jnp.zeros_like(sps[b % 4])
        rss[b % 4][...] = jnp.zeros_like(rss[b % 4])

    def trips(i, _, n):
        for k in range(n):
            trip(nd + 4 * i + k, (nd + k) % 4)
        return 0

    pairs = (last + 6 - nd) // 2
    lax.fori_loop(0, pairs // 2, functools.partial(trips, n=4), 0)
    lax.fori_loop(pairs // 2, (pairs + 1) // 2, functools.partial(trips, n=2), 0)
    acc = acc_ref[...]
    out = acc[:tq]
    for hh in range(1, heads):
        out = jnp.where(in_head[hh], acc[hh * tq:(hh + 1) * tq], out)
    o_ref[...] = out.astype(o_ref.dtype)


def _att_prompt(q, kt, v, bias, dh, tq, tk):
    assert tq in (tk, 2 * tk)
    t, w = q.shape
    rows = (LANES // dh) * tq
    tri = (jnp.arange(tk)[:, None] > jnp.arange(tk)[None, :]).astype(BF16)
    ring = lambda shape, dtype: [pltpu.VMEM(shape, dtype)] * 4
    return pl.pallas_call(
        functools.partial(_att_prompt_body, tq=tq, tk=tk, dh=dh),
        grid=(w // LANES, t // tq),
        in_specs=[pl.BlockSpec(memory_space=pltpu.SMEM),
                  pl.BlockSpec((tq, LANES), lambda p, i: (i, p)),
                  pl.BlockSpec((LANES, t), lambda p, i: (p, 0)),
                  pl.BlockSpec((t, LANES), lambda p, i: (0, p)),
                  _full((tk, tk))],
        out_specs=pl.BlockSpec((tq, LANES), lambda p, i: (i, p)),
        out_shape=jax.ShapeDtypeStruct((t, w), BF16),
        scratch_shapes=(ring((rows, tk), F32) + ring((rows, tk), BF16) + ring((rows, 1), F32)
                        + ring((rows, tk), BF16) + [pltpu.VMEM((rows, LANES), F32), pltpu.VMEM((rows, 1), F32)]),
        compiler_params=_cparams("parallel", "parallel"),
        name="att_prompt",
    )(bias, q, kt, v, tri)


def _att_decode_body(pt_ref, q_ref, bias_ref, tri_ref, *rest, pps, dh):
    kt_refs, vt_refs = rest[:pps], rest[pps:2 * pps]
    o_ref, acc_ref, run_ref = rest[2 * pps:]
    step = pl.program_id(1)
    w, page = acc_ref.shape
    heads = w // dh

    @pl.when(step == 0)
    def _():
        acc_ref[...] = jnp.zeros_like(acc_ref)
        run_ref[...] = jnp.zeros_like(run_ref)

    head_of_lane = lax.broadcasted_iota(jnp.int32, (heads, w), 1) // dh
    own = head_of_lane == lax.broadcasted_iota(jnp.int32, (heads, w), 0)
    qbd = jnp.where(own, jnp.broadcast_to(q_ref[0].astype(F32), (heads, w)), 0.0).astype(BF16)
    bias = bias_ref[...]
    tri = tri_ref[...]
    run = run_ref[...]
    wgt = []
    for r in range(pps):
        lb, sp = _log2_gates(_dot(qbd, kt_refs[r][0].astype(BF16)) + bias)
        sp_hi = sp.astype(BF16)
        sp_lo = (sp - sp_hi.astype(F32)).astype(BF16)
        later = _dot(sp_hi, tri) + _dot(sp_lo, tri)
        wgt.append(jnp.exp2(lb - later - run))
        run = run + jnp.sum(sp, axis=1, keepdims=True)
    run_ref[...] = run
    for hh in range(heads):
        rows = slice(hh * dh, (hh + 1) * dh)
        a = acc_ref[rows, :]
        for r in range(pps):
            a = a + vt_refs[r][0, rows, :] * jnp.broadcast_to(wgt[r][hh:hh + 1, :], (dh, page))
        acc_ref[rows, :] = a

    @pl.when(step == pl.num_programs(1) - 1)
    def _():
        o_ref[0] = jnp.sum(acc_ref[...], axis=1, keepdims=True)


def _att_decode(q, pool_kt, pool_vt, page_table, page_base, bias, dh, pps):
    bsz, w = q.shape
    n_pages = page_table.shape[1]
    page = pool_kt.shape[2]
    heads = w // dh
    assert n_pages % pps == 0
    pt = (page_table + page_base).reshape(-1)
    tri = (jnp.arange(page)[:, None] > jnp.arange(page)[None, :]).astype(BF16)
    bias_b = jnp.broadcast_to((bias * LOG2_E)[:, None], (heads, page))

    def page_spec(r):
        return pl.BlockSpec((1, w, page), lambda b, j, pt_ref: (pt_ref[b * n_pages + n_pages - 1 - (j * pps + r)], 0, 0))

    cache_specs = [page_spec(r) for r in range(pps)]
    out = pl.pallas_call(
        functools.partial(_att_decode_body, pps=pps, dh=dh),
        grid_spec=pltpu.PrefetchScalarGridSpec(
            num_scalar_prefetch=1,
            grid=(bsz, n_pages // pps),
            in_specs=[pl.BlockSpec((1, 1, w), lambda b, j, pt_ref: (b, 0, 0)),
                      pl.BlockSpec((heads, page), lambda b, j, pt_ref: (0, 0)),
                      pl.BlockSpec((page, page), lambda b, j, pt_ref: (0, 0))] + cache_specs + cache_specs,
            out_specs=pl.BlockSpec((1, w, 1), lambda b, j, pt_ref: (b, 0, 0)),
            scratch_shapes=[pltpu.VMEM((w, page), F32), pltpu.VMEM((heads, 1), F32)],
        ),
        out_shape=jax.ShapeDtypeStruct((bsz, w, 1), F32),
        compiler_params=_cparams("parallel", "arbitrary"),
        name="att_decode",
    )(pt, q.reshape(bsz, 1, w), bias_b, tri, *([pool_kt] * pps), *([pool_vt] * pps))
    return out.reshape(bsz, w)


def _merge_body(x_ref, ys_ref, att_ref, pre_ref, post_ref, wglu_ref, wbs_ref, wba_ref, wg_ref, bg_ref, wo_ref,
                o_ref):
    x = x_ref[...]
    d = x.shape[1]
    ws = ys_ref.shape[1]
    u = _rms(x, pre_ref[...]).astype(BF16)
    glu = _dot(jax.nn.gelu(ys_ref[...]).astype(BF16), wglu_ref[...])
    y_s = (glu[:, :ws] * jax.nn.sigmoid(glu[:, ws:])).astype(BF16)
    o_s = _dot(y_s, wbs_ref[...])
    o_a = _dot(att_ref[...].astype(BF16), wba_ref[...])
    g_s = jax.nn.sigmoid(_dot(u, wg_ref[:, :d]) + bg_ref[:, :d])
    g_a = jax.nn.sigmoid(_dot(u, wg_ref[:, d:]) + bg_ref[:, d:])
    merged = (g_s * o_s + g_a * o_a).astype(BF16)
    o_ref[...] = x + _rms(_dot(merged, wo_ref[...]), post_ref[...])


def _merge(x, ys, att, pre_g, post_g, w_glu, w_bs, w_ba, w_gate, b_gate, w_out, tm):
    n, d = x.shape
    row = lambda a: pl.BlockSpec((tm, a.shape[1]), lambda i: (i, 0))
    weights = (pre_g, post_g, w_glu, w_bs, w_ba, w_gate, b_gate, w_out)
    return pl.pallas_call(
        _merge_body,
        grid=(n // tm,),
        in_specs=[row(x), row(ys), row(att)] + [_full(a.shape) for a in weights],
        out_specs=row(x),
        out_shape=jax.ShapeDtypeStruct((n, d), F32),
        compiler_params=_cparams("parallel"),
        name="merge",
    )(x, ys, att, *weights)


def kernel(x_prompt, x_sample, cache_k, cache_v, state_ssm_re, state_ssm_im, page_table, ffn1_pre_g, ffn1_post_g, ffn1_w_up, ffn1_w_down, mix_pre_g, mix_post_g, w_in, sb_bias, lam_re, lam_im, log_dt, b_re, b_im, c_re, c_im, d_skip, w_glu, w_br_ssm, w_br_att, w_gate, b_gate, w_out, ffn2_pre_g, ffn2_post_g, ffn2_w_up, ffn2_w_down):
    bsz_p, t, d = x_prompt.shape
    bsz_s, t_s, _ = x_sample.shape
    assert bsz_p == 1 and t_s == 1
    depth, n_phys, page, heads, dh = cache_k.shape
    _, g, p = lam_re.shape
    c = b_re.shape[3]
    ws, wa = g * c, heads * dh
    q_scale = LOG2_E / math.sqrt(dh)
    tm = min(ROW_TILE, t)
    pps = math.gcd(DEC_PAGES_PER_STEP, page_table.shape[1])

    xp = x_prompt.reshape(t, d)
    xs = x_sample.reshape(bsz_s, d)
    pool_kt = cache_k.transpose(0, 1, 3, 4, 2).reshape(depth * n_phys, wa, page)
    pool_vt = cache_v.transpose(0, 1, 3, 4, 2).reshape(depth * n_phys, wa, page)
    row = lambda a: a.reshape(1, -1)
    zeros_state = jnp.zeros((1, g * p), F32)

    lbr_all, lbi_all, bbr_all, bbi_all = jax.vmap(_s5_discretize)(lam_re, lam_im, log_dt, b_re, b_im)
    mats = jax.vmap(functools.partial(_s5_chunk_mats, n_l=S5_CHUNK))(lbr_all, lbi_all, bbr_all, bbi_all, c_re, c_im)
    step_mats = jax.vmap(_s5_step_mats)(bbr_all, bbi_all, c_re, c_im)

    outs = [[] for _ in range(8)]
    for l in range(depth):
        wb = lambda a: a[l].astype(BF16)
        f1 = (row(ffn1_pre_g[l]), row(ffn1_post_g[l]), wb(ffn1_w_up), wb(ffn1_w_down))
        f2 = (row(ffn2_pre_g[l]), row(ffn2_post_g[l]), wb(ffn2_w_up), wb(ffn2_w_down))
        w_in_l = wb(w_in)
        mrg = (row(mix_pre_g[l]), row(mix_post_g[l]), wb(w_glu), wb(w_br_ssm), wb(w_br_att), wb(w_gate),
               row(b_gate[l]), wb(w_out))

        xp = _ffn(xp, *f1, tm)
        s_in, q, k, v, vb = _mixin(xp, row(mix_pre_g[l]), w_in_l, ws, wa, q_scale, tm)
        y_s, hr, hi = _s5_prompt(s_in, zeros_state, zeros_state, mats, l, d_skip[l], S5_CHUNK, g, p)
        att = _att_prompt(q, k.T.astype(BF16), vb, sb_bias[l], dh, min(ATT_TQ, t), min(ATT_TK, t))
        xp = _merge(xp, y_s, att, *mrg, tm)
        xp = _ffn(xp, *f2, tm)
        outs[0].append(k.reshape(bsz_p, t, heads, dh))
        outs[1].append(v.reshape(bsz_p, t, heads, dh))
        outs[2].append(hr.reshape(bsz_p, g, p))
        outs[3].append(hi.reshape(bsz_p, g, p))

        xs = _ffn(xs, *f1, bsz_s)
        s_in, q, k, v, _ = _mixin(xs, row(mix_pre_g[l]), w_in_l, ws, wa, q_scale, bsz_s)
        y_s, hr, hi = _s5_step(s_in, state_ssm_re[l].reshape(bsz_s, g * p), state_ssm_im[l].reshape(bsz_s, g * p),
                               lbr_all[l], lbi_all[l], step_mats, l, d_skip[l])
        att = _att_decode(q, pool_kt, pool_vt, page_table, l * n_phys, sb_bias[l], dh, pps)
        xs = _merge(xs, y_s, att, *mrg, bsz_s)
        xs = _ffn(xs, *f2, bsz_s)
        outs[4].append(k.reshape(bsz_s, t_s, heads, dh))
        outs[5].append(v.reshape(bsz_s, t_s, heads, dh))
        outs[6].append(hr.reshape(bsz_s, g, p))
        outs[7].append(hi.reshape(bsz_s, g, p))

    st = [jnp.stack(o) for o in outs]
    return (xp.reshape(bsz_p, t, d), xs.reshape(bsz_s, t_s, d), st[0], st[1], st[2], st[3], st[4], st[5], st[6], st[7])
```

```python
import functools
import math

import jax
import jax.numpy as jnp
from jax import lax
from jax.experimental import pallas as pl
from jax.experimental.pallas import tpu as pltpu

F32 = jnp.float32
BF16 = jnp.bfloat16
RMS_EPS = 1e-6
LOG2_E = 1.4426950408889634
EXP2_CLAMP = 126.0
MASKED_LOG2 = -1e30

LANES = 128
VMEM_LIMIT = 56 * 1024 * 1024

S5_CHUNK = 8
S5_ROWS = 256
ATT_TQ = 256
ATT_TK = 256
ROW_TILE = 512
DEC_PAGES_PER_STEP = 8


def _cparams(*sem):
    return pltpu.CompilerParams(dimension_semantics=sem, vmem_limit_bytes=VMEM_LIMIT)


def _full(shape):
    n = len(shape)
    return pl.BlockSpec(shape, lambda *_: (0,) * n, pipeline_mode=pl.Buffered(1))


def _rms(x, g):
    return x * lax.rsqrt(jnp.mean(x * x, axis=-1, keepdims=True) + RMS_EPS) * g


def _dot(a, b):
    return jnp.dot(a, b, preferred_element_type=F32)


def _dot3(a, b):
    ah = a.astype(BF16)
    al = (a - ah.astype(F32)).astype(BF16)
    bh = b.astype(BF16)
    bl = (b - bh.astype(F32)).astype(BF16)
    return _dot(ah, bh) + _dot(ah, bl) + _dot(al, bh)


def _ffn_body(x_ref, pre_ref, post_ref, wup_ref, wdn_ref, o_ref, *, d_ff, fc):
    x = x_ref[...]
    u = _rms(x, pre_ref[...]).astype(BF16)
    acc = None
    for c in range(d_ff // fc):
        a = _dot(u, wup_ref[:, c * fc:(c + 1) * fc])
        b = _dot(u, wup_ref[:, d_ff + c * fc:d_ff + (c + 1) * fc])
        h = (a * jax.nn.sigmoid(a) * b).astype(BF16)
        p = _dot(h, wdn_ref[c * fc:(c + 1) * fc, :])
        acc = p if acc is None else acc + p
    o_ref[...] = x + 0.5 * _rms(acc, post_ref[...])


def _ffn(x, pre_g, post_g, w_up, w_dn, tm):
    n, d = x.shape
    d_ff = w_dn.shape[0]
    fc = d_ff // 2 if (d_ff // 2) % LANES == 0 else d_ff
    row = pl.BlockSpec((tm, d), lambda i: (i, 0))
    return pl.pallas_call(
        functools.partial(_ffn_body, d_ff=d_ff, fc=fc),
        grid=(n // tm,),
        in_specs=[row, _full((1, d)), _full((1, d)), _full((d, 2 * d_ff)), _full((d_ff, d))],
        out_specs=row,
        out_shape=jax.ShapeDtypeStruct((n, d), F32),
        compiler_params=_cparams("parallel"),
        name="ffn",
    )(x, pre_g, post_g, w_up, w_dn)


def _mixin_body(x_ref, g_ref, win_ref, s_ref, q_ref, k_ref, v_ref, vb_ref, *, ws, wa, q_scale):
    u = _rms(x_ref[...], g_ref[...]).astype(BF16)
    s_ref[...] = _dot(u, win_ref[:, :ws])
    q_ref[...] = (_dot(u, win_ref[:, ws:ws + wa]) * q_scale).astype(BF16)
    k_ref[...] = _dot(u, win_ref[:, ws + wa:ws + 2 * wa])
    v = _dot(u, win_ref[:, ws + 2 * wa:])
    v_ref[...] = v
    vb_ref[...] = v.astype(BF16)


def _mixin(x, g, w_in, ws, wa, q_scale, tm):
    n, d = x.shape
    row = lambda w: pl.BlockSpec((tm, w), lambda i: (i, 0))
    return pl.pallas_call(
        functools.partial(_mixin_body, ws=ws, wa=wa, q_scale=q_scale),
        grid=(n // tm,),
        in_specs=[row(d), _full((1, d)), _full(w_in.shape)],
        out_specs=[row(ws), row(wa), row(wa), row(wa), row(wa)],
        out_shape=[jax.ShapeDtypeStruct((n, ws), F32), jax.ShapeDtypeStruct((n, wa), BF16),
                   jax.ShapeDtypeStruct((n, wa), F32), jax.ShapeDtypeStruct((n, wa), F32),
                   jax.ShapeDtypeStruct((n, wa), BF16)],
        compiler_params=_cparams("parallel"),
        name="mixer_in",
    )(x, g, w_in)


def _s5_discretize(lam_re, lam_im, log_dt, b_re, b_im):
    dt = jnp.exp(log_dt)[:, None]
    mag = jnp.exp(lam_re * dt)
    lbr, lbi = mag * jnp.cos(lam_im * dt), mag * jnp.sin(lam_im * dt)
    nr, ni = lbr - 1.0, lbi
    den = lam_re * lam_re + lam_im * lam_im
    fr = ((nr * lam_re + ni * lam_im) / den)[:, :, None]
    fi = ((ni * lam_re - nr * lam_im) / den)[:, :, None]
    return lbr, lbi, fr * b_re - fi * b_im, fr * b_im + fi * b_re


def _s5_chunk_mats(lbr, lbi, bbr, bbi, c_re, c_im, n_l):
    g, p = lbr.shape
    c = bbr.shape[2]
    gb = LANES // c
    nb = g // gb
    eye = jnp.eye(gb, dtype=F32)
    hi = lax.Precision.HIGHEST
    pr, pi = [jnp.ones_like(lbr)], [jnp.zeros_like(lbr)]
    for _ in range(n_l):
        pr, pi = pr + [pr[-1] * lbr - pi[-1] * lbi], pi + [pr[-1] * lbi + pi[-1] * lbr]
    rev_r, rev_i = jnp.stack(pr[n_l - 1::-1]), jnp.stack(pi[n_l - 1::-1])
    pr, pi = jnp.stack(pr), jnp.stack(pi)
    clr = c_re[None] * pr[:, :, None, :] - c_im[None] * pi[:, :, None, :]
    cli = c_re[None] * pi[:, :, None, :] + c_im[None] * pr[:, :, None, :]
    gj = (jnp.einsum('jgcp,gpd->jgcd', clr[:n_l], bbr, precision=hi)
          - jnp.einsum('jgcp,gpd->jgcd', cli[:n_l], bbi, precision=hi))
    gr = gj[jnp.arange(n_l - 1, -1, -1)].reshape(n_l, nb, gb, c, c).transpose(1, 0, 2, 4, 3)
    r = gr[:, :, :, :, None, :] * eye[None, None, :, None, :, None]
    r = r.reshape(nb, n_l * LANES, LANES).astype(BF16)
    br = rev_r[:, :, :, None] * bbr[None] - rev_i[:, :, :, None] * bbi[None]
    bi = rev_r[:, :, :, None] * bbi[None] + rev_i[:, :, :, None] * bbr[None]

    def bd_b(b):
        b = b.reshape(n_l, nb, gb, p, c).transpose(1, 0, 2, 4, 3)
        b = b[:, :, :, :, None, :] * eye[None, None, :, None, :, None]
        return b.reshape(nb, n_l * LANES, gb * p)

    bm = jnp.concatenate([bd_b(br), bd_b(bi)], axis=-1).astype(BF16)

    def bd_c(m_):
        m_ = m_.reshape(n_l, nb, gb, c, p).transpose(1, 2, 4, 0, 3)
        m_ = m_[:, :, :, :, None, :] * eye[None, :, None, None, :, None]
        return m_.reshape(nb, gb * p, n_l * LANES).astype(BF16)

    return r, bm, bd_c(clr[1:]), bd_c(-cli[1:]), pr[n_l].reshape(1, g * p), pi[n_l].reshape(1, g * p)


def _gather_steps(u_ref, ucat_ref, n_l):
    for s in range(n_l):
        ucat_ref[:, s * LANES:(s + 1) * LANES] = u_ref[:, s, :].astype(BF16)


def _s5_local_body(u_ref, b_ref, sre_ref, sim_ref, ucat_ref, *, n_l):
    _gather_steps(u_ref, ucat_ref, n_l)
    st = _dot(ucat_ref[...], b_ref[...])
    half = st.shape[1] // 2
    sre_ref[...] = st[:, :half]
    sim_ref[...] = st[:, half:]


def _s5_scan_body(sre_ref, sim_ref, ar_ref, ai_ref, h0r_ref, h0i_ref, hr_ref, hi_ref, fr_ref, fi_ref, *, nc):
    ar, ai = ar_ref[...], ai_ref[...]

    def step(k, carry):
        hr, hi = carry
        hr_ref[pl.ds(k, 1), :] = hr
        hi_ref[pl.ds(k, 1), :] = hi
        sr = sre_ref[pl.ds(k, 1), :]
        si = sim_ref[pl.ds(k, 1), :]
        return ar * hr - ai * hi + sr, ar * hi + ai * hr + si

    hr, hi = lax.fori_loop(0, nc, step, (h0r_ref[...], h0i_ref[...]), unroll=8)
    fr_ref[...] = hr
    fi_ref[...] = hi


def _s5_out_body(u_ref, r_ref, hr_ref, hi_ref, cr_ref, ci_ref, d_ref, y_ref, ucat_ref, *, n_l):
    _gather_steps(u_ref, ucat_ref, n_l)
    y = _dot(hr_ref[...].astype(BF16), cr_ref[...]) + _dot(hi_ref[...].astype(BF16), ci_ref[...])
    d = d_ref[...]
    for t in range(n_l):
        conv = _dot(ucat_ref[:, :(t + 1) * LANES], r_ref[(n_l - 1 - t) * LANES:, :])
        y_ref[:, t, :] = conv + y[:, t * LANES:(t + 1) * LANES] + d * u_ref[:, t, :]


def _s5_prompt(s_in, h0r, h0i, mats, layer, d_skip, n_l, g, p):
    t, ws = s_in.shape
    r, bm, cr, ci, alr, ali = mats
    nb, nc = ws // LANES, t // n_l
    nr = min(S5_ROWS, nc)
    wp = g * p // nb
    u = s_in.reshape(nc, n_l, ws)
    u_spec = pl.BlockSpec((nr, n_l, LANES), lambda j, r: (r, 0, j))
    st_spec = pl.BlockSpec((nr, wp), lambda j, r: (r, j))
    mat = lambda a: pl.BlockSpec((None, None) + a.shape[2:], lambda j, r: (layer, j, 0, 0))
    ucat = pltpu.VMEM((nr, n_l * LANES), BF16)
    state = jax.ShapeDtypeStruct((nc, g * p), F32)
    sre, sim = pl.pallas_call(
        functools.partial(_s5_local_body, n_l=n_l),
        grid=(nb, nc // nr),
        in_specs=[u_spec, mat(bm)],
        out_specs=[st_spec, st_spec],
        out_shape=[state, state],
        scratch_shapes=[ucat],
        compiler_params=_cparams("parallel", "parallel"),
        name="s5_local",
    )(u, bm)
    sw = 4 * LANES
    col = lambda r, w: pl.BlockSpec((r, w), lambda i: (0, i))
    last = jax.ShapeDtypeStruct((1, g * p), F32)
    hr, hi, fr, fi = pl.pallas_call(
        functools.partial(_s5_scan_body, nc=nc),
        grid=(g * p // sw,),
        in_specs=[col(nc, sw), col(nc, sw), col(1, sw), col(1, sw), col(1, sw), col(1, sw)],
        out_specs=[col(nc, sw), col(nc, sw), col(1, sw), col(1, sw)],
        out_shape=[state, state, last, last],
        compiler_params=_cparams("parallel"),
        name="s5_scan",
    )(sre, sim, alr[layer], ali[layer], h0r, h0i)
    y = pl.pallas_call(
        functools.partial(_s5_out_body, n_l=n_l),
        grid=(nb, nc // nr),
        in_specs=[u_spec, mat(r), st_spec, st_spec, mat(cr), mat(ci),
                  pl.BlockSpec((1, LANES), lambda j, r: (0, j))],
        out_specs=u_spec,
        out_shape=jax.ShapeDtypeStruct((nc, n_l, ws), F32),
        scratch_shapes=[ucat],
        compiler_params=_cparams("parallel", "parallel"),
        name="s5_out",
    )(u, r, hr, hi, cr, ci, d_skip.reshape(1, ws))
    return y.reshape(t, ws), fr, fi


def _s5_step_body(u_ref, h0r_ref, h0i_ref, lr_ref, li_ref, br_ref, bi_ref, cr_ref, ci_ref, d_ref,
                  y_ref, hr_ref, hi_ref):
    u = u_ref[...]
    h0r, h0i, lr, li = h0r_ref[...], h0i_ref[...], lr_ref[...], li_ref[...]
    hr = lr * h0r - li * h0i + _dot3(u, br_ref[...])
    hi = lr * h0i + li * h0r + _dot3(u, bi_ref[...])
    hr_ref[...] = hr
    hi_ref[...] = hi
    y_ref[...] = _dot3(hr, cr_ref[...]) - _dot3(hi, ci_ref[...]) + d_ref[...] * u


def _s5_step_mats(bbr, bbi, c_re, c_im):
    g, p, c = bbr.shape
    gb = LANES // c
    nb = g // gb
    eye = jnp.eye(gb, dtype=F32)

    def bd_in(b):
        return jnp.einsum('kapc,ab->kacbp', b.reshape(nb, gb, p, c), eye).reshape(nb, gb * c, gb * p)

    def bd_out(m):
        return jnp.einsum('kacp,ab->kapbc', m.reshape(nb, gb, c, p), eye).reshape(nb, gb * p, gb * c)

    return bd_in(bbr), bd_in(bbi), bd_out(c_re), bd_out(c_im)


def _s5_step(u, h0r, h0i, lbr, lbi, step_mats, layer, d_skip):
    bsz = u.shape[0]
    _, nb, wc, wp = step_mats[0].shape
    g_p, g_c = nb * wp, nb * wc
    blk = lambda r, w: pl.BlockSpec((r, w), lambda k: (0, k))
    grp = lambda r, w: pl.BlockSpec((None, None, r, w), lambda k: (layer, k, 0, 0))
    return pl.pallas_call(
        _s5_step_body,
        grid=(nb,),
        in_specs=[blk(bsz, wc), blk(bsz, wp), blk(bsz, wp), blk(1, wp), blk(1, wp),
                  grp(wc, wp), grp(wc, wp), grp(wp, wc), grp(wp, wc), blk(1, wc)],
        out_specs=[blk(bsz, wc), blk(bsz, wp), blk(bsz, wp)],
        out_shape=[jax.ShapeDtypeStruct((bsz, g_c), F32), jax.ShapeDtypeStruct((bsz, g_p), F32),
                   jax.ShapeDtypeStruct((bsz, g_p), F32)],
        compiler_params=_cparams("parallel"),
        name="s5_step",
    )(u, h0r, h0i, lbr.reshape(1, g_p), lbi.reshape(1, g_p), *step_mats, d_skip.reshape(1, g_c))


def _log2_gates(z):
    sp = jnp.maximum(jnp.log(1.0 + jnp.exp2(jnp.minimum(z, EXP2_CLAMP))) * LOG2_E, z)
    return z - sp, sp


def _att_prompt_body(bias_ref, q_ref, kt_ref, v_ref, tri_ref, o_ref,
                     lb0, lb1, lb2, lb3, sp0, sp1, sp2, sp3, rs0, rs1, rs2, rs3, w0, w1, w2, w3, acc_ref, run_ref,
                     *, tq, tk, dh):
    lbs, sps, rss, wbs = (lb0, lb1, lb2, lb3), (sp0, sp1, sp2, sp3), (rs0, rs1, rs2, rs3), (w0, w1, w2, w3)
    pair = pl.program_id(0)
    qi = pl.program_id(1)
    q = q_ref[...]
    heads = q.shape[1] // dh
    lane = lax.broadcasted_iota(jnp.int32, q.shape, 1)
    in_head = [(lane >= hh * dh) & (lane < (hh + 1) * dh) for hh in range(heads)]
    q2 = jnp.concatenate([jnp.where(m, q, jnp.zeros_like(q)) for m in in_head], axis=0)
    bias = [bias_ref[pair * heads + hh] * LOG2_E for hh in range(heads)]

    nd = tq // tk
    last = (qi + 1) * nd - 1

    def keys_of(b):
        j = jnp.clip(last - b, 0, last)
        return pl.ds(pl.multiple_of(j * tk, tk), tk)

    def scores(slot, s, penalty, key_offset):
        for hh in range(heads):
            r = slice(hh * tq, (hh + 1) * tq)
            lb, sp = _log2_gates(s[r, :] + (bias[hh] + penalty))
            if key_offset is not None:
                col = lax.broadcasted_iota(jnp.int32, sp.shape, 1) + key_offset
                valid = col < lax.broadcasted_iota(jnp.int32, sp.shape, 0)
                sp = jnp.where(valid, sp, 0.0)
                lb = jnp.where(valid, lb, MASKED_LOG2)
            lbs[slot][r, :] = lb
            sps[slot][r, :] = sp.astype(BF16)
            rss[slot][r, :] = sp[:, 0:1]

    def trip(t, slot):
        prev = (slot - 2) % 4
        later = _dot(sps[prev][...], tri_ref[...])
        s = _dot(q2, kt_ref[:, keys_of(t)])
        acc_ref[...] += _dot(wbs[slot][...], v_ref[keys_of(t - 4), :])
        wbs[prev][...] = jnp.exp2(lbs[prev][...] - later - run_ref[...]).astype(BF16)
        run_ref[...] += rss[prev][...] + later[:, 0:1]
        scores(slot, s, jnp.where(t > last, MASKED_LOG2, 0.0), None)

    acc_ref[...] = jnp.zeros_like(acc_ref)
    run_ref[...] = jnp.zeros_like(run_ref)
    for w in wbs:
        w[...] = jnp.zeros_like(w)
    for b in range(nd):
        scores(b, _dot(q2, kt_ref[:, keys_of(b)]), 0.0, (nd - 1 - b) * tk)
    for b in range(nd - 2, 0):
        lbs[b % 4][...] = jnp.full(lbs[b % 4].shape, MASKED_LOG2, F32)
        sps[b % 4][...] = jnp.zeros_like(sps[b % 4])
        rss[b % 4][...] = jnp.zeros_like(rss[b % 4])

    def trips(i, _, n):
        for k in range(n):
            trip(nd + 4 * i + k, (nd + k) % 4)
        return 0

    pairs = (last + 6 - nd) // 2
    lax.fori_loop(0, pairs // 2, functools.partial(trips, n=4), 0)
    lax.fori_loop(pairs // 2, (pairs + 1) // 2, functools.partial(trips, n=2), 0)
    acc = acc_ref[...]
    out = acc[:tq]
    for hh in range(1, heads):
        out = jnp.where(in_head[hh], acc[hh * tq:(hh + 1) * tq], out)
    o_ref[...] = out.astype(o_ref.dtype)


def _att_prompt(q, kt, v, bias, dh, tq, tk):
    assert tq in (tk, 2 * tk)
    t, w = q.shape
    rows = (LANES // dh) * tq
    tri = (jnp.arange(tk)[:, None] > jnp.arange(tk)[None, :]).astype(BF16)
    ring = lambda shape, dtype: [pltpu.VMEM(shape, dtype)] * 4
    return pl.pallas_call(
        functools.partial(_att_prompt_body, tq=tq, tk=tk, dh=dh),
        grid=(w // LANES, t // tq),
        in_specs=[pl.BlockSpec(memory_space=pltpu.SMEM),
                  pl.BlockSpec((tq, LANES), lambda p, i: (i, p)),
                  pl.BlockSpec((LANES, t), lambda p, i: (p, 0)),
                  pl.BlockSpec((t, LANES), lambda p, i: (0, p)),
                  _full((tk, tk))],
        out_specs=pl.BlockSpec((tq, LANES), lambda p, i: (i, p)),
        out_shape=jax.ShapeDtypeStruct((t, w), BF16),
        scratch_shapes=(ring((rows, tk), F32) + ring((rows, tk), BF16) + ring((rows, 1), F32)
                        + ring((rows, tk), BF16) + [pltpu.VMEM((rows, LANES), F32), pltpu.VMEM((rows, 1), F32)]),
        compiler_params=_cparams("parallel", "parallel"),
        name="att_prompt",
    )(bias, q, kt, v, tri)


def _att_decode_body(pt_ref, q_ref, bias_ref, tri_ref, *rest, pps, dh):
    kt_refs, vt_refs = rest[:pps], rest[pps:2 * pps]
    o_ref, acc_ref, run_ref = rest[2 * pps:]
    step = pl.program_id(1)
    w, page = acc_ref.shape
    heads = w // dh

    @pl.when(step == 0)
    def _():
        acc_ref[...] = jnp.zeros_like(acc_ref)
        run_ref[...] = jnp.zeros_like(run_ref)

    head_of_lane = lax.broadcasted_iota(jnp.int32, (heads, w), 1) // dh
    own = head_of_lane == lax.broadcasted_iota(jnp.int32, (heads, w), 0)
    qbd = jnp.where(own, jnp.broadcast_to(q_ref[0].astype(F32), (heads, w)), 0.0).astype(BF16)
    bias = bias_ref[...]
    tri = tri_ref[...]
    run = run_ref[...]
    wgt = []
    for r in range(pps):
        lb, sp = _log2_gates(_dot(qbd, kt_refs[r][0].astype(BF16)) + bias)
        sp_hi = sp.astype(BF16)
        sp_lo = (sp - sp_hi.astype(F32)).astype(BF16)
        later = _dot(sp_hi, tri) + _dot(sp_lo, tri)
        wgt.append(jnp.exp2(lb - later - run))
        run = run + jnp.sum(sp, axis=1, keepdims=True)
    run_ref[...] = run
    for hh in range(heads):
        rows = slice(hh * dh, (hh + 1) * dh)
        a = acc_ref[rows, :]
        for r in range(pps):
            a = a + vt_refs[r][0, rows, :] * jnp.broadcast_to(wgt[r][hh:hh + 1, :], (dh, page))
        acc_ref[rows, :] = a

    @pl.when(step == pl.num_programs(1) - 1)
    def _():
        o_ref[0] = jnp.sum(acc_ref[...], axis=1, keepdims=True)


def _att_decode(q, pool_kt, pool_vt, page_table, page_base, bias, dh, pps):
    bsz, w = q.shape
    n_pages = page_table.shape[1]
    page = pool_kt.shape[2]
    heads = w // dh
    assert n_pages % pps == 0
    pt = (page_table + page_base).reshape(-1)
    tri = (jnp.arange(page)[:, None] > jnp.arange(page)[None, :]).astype(BF16)
    bias_b = jnp.broadcast_to((bias * LOG2_E)[:, None], (heads, page))

    def page_spec(r):
        return pl.BlockSpec((1, w, page), lambda b, j, pt_ref: (pt_ref[b * n_pages + n_pages - 1 - (j * pps + r)], 0, 0))

    cache_specs = [page_spec(r) for r in range(pps)]
    out = pl.pallas_call(
        functools.partial(_att_decode_body, pps=pps, dh=dh),
        grid_spec=pltpu.PrefetchScalarGridSpec(
            num_scalar_prefetch=1,
            grid=(bsz, n_pages // pps),
            in_specs=[pl.BlockSpec((1, 1, w), lambda b, j, pt_ref: (b, 0, 0)),
                      pl.BlockSpec((heads, page), lambda b, j, pt_ref: (0, 0)),
                      pl.BlockSpec((page, page), lambda b, j, pt_ref: (0, 0))] + cache_specs + cache_specs,
            out_specs=pl.BlockSpec((1, w, 1), lambda b, j, pt_ref: (b, 0, 0)),
            scratch_shapes=[pltpu.VMEM((w, page), F32), pltpu.VMEM((heads, 1), F32)],
        ),
        out_shape=jax.ShapeDtypeStruct((bsz, w, 1), F32),
        compiler_params=_cparams("parallel", "arbitrary"),
        name="att_decode",
    )(pt, q.reshape(bsz, 1, w), bias_b, tri, *([pool_kt] * pps), *([pool_vt] * pps))
    return out.reshape(bsz, w)


def _merge_body(x_ref, ys_ref, att_ref, pre_ref, post_ref, wglu_ref, wbs_ref, wba_ref, wg_ref, bg_ref, wo_ref,
                o_ref):
    x = x_ref[...]
    d = x.shape[1]
    ws = ys_ref.shape[1]
    u = _rms(x, pre_ref[...]).astype(BF16)
    glu = _dot(jax.nn.gelu(ys_ref[...]).astype(BF16), wglu_ref[...])
    y_s = (glu[:, :ws] * jax.nn.sigmoid(glu[:, ws:])).astype(BF16)
    o_s = _dot(y_s, wbs_ref[...])
    o_a = _dot(att_ref[...].astype(BF16), wba_ref[...])
    g_s = jax.nn.sigmoid(_dot(u, wg_ref[:, :d]) + bg_ref[:, :d])
    g_a = jax.nn.sigmoid(_dot(u, wg_ref[:, d:]) + bg_ref[:, d:])
    merged = (g_s * o_s + g_a * o_a).astype(BF16)
    o_ref[...] = x + _rms(_dot(merged, wo_ref[...]), post_ref[...])


def _merge(x, ys, att, pre_g, post_g, w_glu, w_bs, w_ba, w_gate, b_gate, w_out, tm):
    n, d = x.shape
    row = lambda a: pl.BlockSpec((tm, a.shape[1]), lambda i: (i, 0))
    weights = (pre_g, post_g, w_glu, w_bs, w_ba, w_gate, b_gate, w_out)
    return pl.pallas_call(
        _merge_body,
        grid=(n // tm,),
        in_specs=[row(x), row(ys), row(att)] + [_full(a.shape) for a in weights],
        out_specs=row(x),
        out_shape=jax.ShapeDtypeStruct((n, d), F32),
        compiler_params=_cparams("parallel"),
        name="merge",
    )(x, ys, att, *weights)


def kernel(x_prompt, x_sample, cache_k, cache_v, state_ssm_re, state_ssm_im, page_table, ffn1_pre_g, ffn1_post_g, ffn1_w_up, ffn1_w_down, mix_pre_g, mix_post_g, w_in, sb_bias, lam_re, lam_im, log_dt, b_re, b_im, c_re, c_im, d_skip, w_glu, w_br_ssm, w_br_att, w_gate, b_gate, w_out, ffn2_pre_g, ffn2_post_g, ffn2_w_up, ffn2_w_down):
    bsz_p, t, d = x_prompt.shape
    bsz_s, t_s, _ = x_sample.shape
    assert bsz_p == 1 and t_s == 1
    depth, n_phys, page, heads, dh = cache_k.shape
    _, g, p = lam_re.shape
    c = b_re.shape[3]
    ws, wa = g * c, heads * dh
    q_scale = LOG2_E / math.sqrt(dh)
    tm = min(ROW_TILE, t)
    pps = math.gcd(DEC_PAGES_PER_STEP, page_table.shape[1])

    xp = x_prompt.reshape(t, d)
    xs = x_sample.reshape(bsz_s, d)
    pool_kt = cache_k.transpose(0, 1, 3, 4, 2).reshape(depth * n_phys, wa, page)
    pool_vt = cache_v.transpose(0, 1, 3, 4, 2).reshape(depth * n_phys, wa, page)
    row = lambda a: a.reshape(1, -1)
    zeros_state = jnp.zeros((1, g * p), F32)

    lbr_all, lbi_all, bbr_all, bbi_all = jax.vmap(_s5_discretize)(lam_re, lam_im, log_dt, b_re, b_im)
    mats = jax.vmap(functools.partial(_s5_chunk_mats, n_l=S5_CHUNK))(lbr_all, lbi_all, bbr_all, bbi_all, c_re, c_im)
    step_mats = jax.vmap(_s5_step_mats)(bbr_all, bbi_all, c_re, c_im)

    outs = [[] for _ in range(8)]
    for l in range(depth):
        wb = lambda a: a[l].astype(BF16)
        f1 = (row(ffn1_pre_g[l]), row(ffn1_post_g[l]), wb(ffn1_w_up), wb(ffn1_w_down))
        f2 = (row(ffn2_pre_g[l]), row(ffn2_post_g[l]), wb(ffn2_w_up), wb(ffn2_w_down))
        w_in_l = wb(w_in)
        mrg = (row(mix_pre_g[l]), row(mix_post_g[l]), wb(w_glu), wb(w_br_ssm), wb(w_br_att), wb(w_gate),
               row(b_gate[l]), wb(w_out))

        xp = _ffn(xp, *f1, tm)
        s_in, q, k, v, vb = _mixin(xp, row(mix_pre_g[l]), w_in_l, ws, wa, q_scale, tm)
        y_s, hr, hi = _s5_prompt(s_in, zeros_state, zeros_state, mats, l, d_skip[l], S5_CHUNK, g, p)
        att = _att_prompt(q, k.T.astype(BF16), vb, sb_bias[l], dh, min(ATT_TQ, t), min(ATT_TK, t))
        xp = _merge(xp, y_s, att, *mrg, tm)
        xp = _ffn(xp, *f2, tm)
        outs[0].append(k.reshape(bsz_p, t, heads, dh))
        outs[1].append(v.reshape(bsz_p, t, heads, dh))
        outs[2].append(hr.reshape(bsz_p, g, p))
        outs[3].append(hi.reshape(bsz_p, g, p))

        xs = _ffn(xs, *f1, bsz_s)
        s_in, q, k, v, _ = _mixin(xs, row(mix_pre_g[l]), w_in_l, ws, wa, q_scale, bsz_s)
        y_s, hr, hi = _s5_step(s_in, state_ssm_re[l].reshape(bsz_s, g * p), state_ssm_im[l].reshape(bsz_s, g * p),
                               lbr_all[l], lbi_all[l], step_mats, l, d_skip[l])
        att = _att_decode(q, pool_kt, pool_vt, page_table, l * n_phys, sb_bias[l], dh, pps)
        xs = _merge(xs, y_s, att, *mrg, bsz_s)
        xs = _ffn(xs, *f2, bsz_s)
        outs[4].append(k.reshape(bsz_s, t_s, heads, dh))
        outs[5].append(v.reshape(bsz_s, t_s, heads, dh))
        outs[6].append(hr.reshape(bsz_s, g, p))
        outs[7].append(hi.reshape(bsz_s, g, p))

    st = [jnp.stack(o) for o in outs]
    return (xp.reshape(bsz_p, t, d), xs.reshape(bsz_s, t_s, d), st[0], st[1], st[2], st[3], st[4], st[5], st[6], st[7])
```
